```python
import jax, jax.numpy as jnp
from jax import lax
import numpy as np

D_MODEL = 1024
BATCH = 4
SEQ = 8192
DEPTH = 1
DEC_BATCH = 8
DEC_SEQ = 32
PAST_LEN = 2048

CHUNK = 64
N_HEADS_ATT = 16
HEAD_DIM = 64
ATT_WIDTH = N_HEADS_ATT * HEAD_DIM
ATT_SCALE = HEAD_DIM ** -0.5
Q_BLOCK = 128
SSM_EXPAND = 2
D_INNER = SSM_EXPAND * D_MODEL
SSM_HEAD_DIM = 64
N_HEADS_SSM = D_INNER // SSM_HEAD_DIM
N_GROUPS = 8
D_STATE = 128
CONV_W = 4
CONV_DIM = D_INNER + 2 * N_GROUPS * D_STATE
D_FF = 4 * D_MODEL
D_PROJ = 3 * ATT_WIDTH + N_HEADS_ATT + D_INNER + CONV_DIM + N_HEADS_SSM + 2 * D_MODEL
EPS = 1e-6

kernel_name = 'fox_ssd_hybrid_stream_step'


def _rmsnorm(x, w):
    xf = x.astype(jnp.float32)
    y = xf * lax.rsqrt(jnp.mean(xf * xf, axis=-1, keepdims=True) + EPS)
    return (y * w.astype(jnp.float32)).astype(x.dtype)


def _split_proj(proj):
    sizes = (ATT_WIDTH, ATT_WIDTH, ATT_WIDTH, N_HEADS_ATT, D_INNER, CONV_DIM, N_HEADS_SSM, D_MODEL, D_MODEL)
    parts = []
    start = 0
    for s in sizes:
        parts.append(proj[..., start:start + s])
        start += s
    return parts


def _fox_attend(q, cq, qpos, k, v, ck, kpos):
    s = jnp.einsum('bqhd,bkhd->bhqk', q, k, preferred_element_type=jnp.float32) * ATT_SCALE
    bias = jnp.swapaxes(cq, 1, 2)[:, :, :, None] - jnp.swapaxes(ck, 1, 2)[:, :, None, :]
    mask = kpos[None, :] <= qpos[:, None]
    s = jnp.where(mask, s + bias, -jnp.inf)
    p = jax.nn.softmax(s, axis=-1)
    return jnp.einsum('bhqk,bkhd->bqhd', p.astype(v.dtype), v)


def _fox_prompt(q, k, v, logf):
    b, T, H, d = q.shape
    c = jnp.cumsum(logf, axis=1)
    pos = jnp.arange(T)
    nb = T // Q_BLOCK
    qs = q.reshape(b, nb, Q_BLOCK, H, d).swapaxes(0, 1)
    cs = c.reshape(b, nb, Q_BLOCK, H).swapaxes(0, 1)
    ps = pos.reshape(nb, Q_BLOCK)

    def blk(args):
        qb, cb, pb = args
        return _fox_attend(qb, cb, pb, k, v, c, pos)

    o = lax.map(blk, (qs, cs, ps))
    return o.swapaxes(0, 1).reshape(b, T, H, d)


def _causal_conv(xbc, conv_past, conv_w, conv_b):
    T = xbc.shape[1]
    xin = jnp.concatenate([conv_past.astype(xbc.dtype), xbc], axis=1)
    out = conv_b
    for i in range(CONV_W):
        out = out + xin[:, i:i + T] * conv_w[i]
    return jax.nn.silu(out), xin[:, T:]


def _ssd(x, dt, A, Bm, Cm, h0):
    b, T, H, P = x.shape
    G, N = Bm.shape[2], Bm.shape[3]
    R = H // G
    L = min(CHUNK, T)
    nc = T // L
    f32 = jnp.float32
    xc = x.astype(f32).reshape(b, nc, L, G, R, P)
    dtc = dt.astype(f32).reshape(b, nc, L, G, R)
    Bc = Bm.astype(f32).reshape(b, nc, L, G, N)
    Cc = Cm.astype(f32).reshape(b, nc, L, G, N)
    acum = jnp.cumsum(dtc * A.astype(f32).reshape(G, R), axis=2)
    tri = jnp.tril(jnp.ones((L, L), dtype=bool))[:, :, None, None]
    seg = acum[:, :, :, None] - acum[:, :, None, :]
    decay = jnp.exp(jnp.where(tri, seg, -jnp.inf))
    cb = jnp.einsum('bclgn,bcsgn->bclsg', Cc, Bc)
    m = cb[..., None] * decay * dtc[:, :, None]
    y_diag = jnp.einsum('bclsgr,bcsgrp->bclgrp', m, xc)
    w_state = jnp.exp(acum[:, :, -1:] - acum) * dtc
    states = jnp.einsum('bclgn,bclgrp->bcgrpn', Bc, xc * w_state[..., None])
    chunk_decay = jnp.exp(acum[:, :, -1])

    def step(h, inp):
        dec, st = inp
        return h * dec[..., None, None] + st, h

    h_last, h_prev = lax.scan(step, h0.astype(f32).reshape(b, G, R, P, N),
                              (jnp.moveaxis(chunk_decay, 1, 0), jnp.moveaxis(states, 1, 0)))
    h_prev = jnp.moveaxis(h_prev, 0, 1)
    y_off = jnp.einsum('bclgn,bcgrpn->bclgrp', Cc, h_prev) * jnp.exp(acum)[..., None]
    y = (y_diag + y_off).reshape(b, T, H, P)
    return y, h_last.reshape(b, H, P, N)


def _layer(x, past, params):
    (norm1_w, w_in, b_f, q_norm_w, k_norm_w, conv_w, conv_b, dt_bias, A_log, D_skip,
     ssm_norm_w, w_pa, w_pb, w_out, norm2_w, w_up, w_down) = params
    f32 = jnp.float32
    b, T, _ = x.shape
    xn = _rmsnorm(x, norm1_w)
    proj = xn @ w_in
    q, k, v, f_raw, z, xbc, dt_raw, g_att, g_ssm = _split_proj(proj)
    q = _rmsnorm(q.reshape(b, T, N_HEADS_ATT, HEAD_DIM), q_norm_w)
    k = _rmsnorm(k.reshape(b, T, N_HEADS_ATT, HEAD_DIM), k_norm_w)
    v = v.reshape(b, T, N_HEADS_ATT, HEAD_DIM)
    logf = jax.nn.log_sigmoid(f_raw.astype(f32) + b_f.astype(f32))
    if past is None:
        att = _fox_prompt(q, k, v, logf)
        conv_past = jnp.zeros((b, CONV_W - 1, CONV_DIM), x.dtype)
        h0 = jnp.zeros((b, N_HEADS_SSM, SSM_HEAD_DIM, D_STATE), f32)
    else:
        k_past, v_past, logf_past, conv_past, h0 = past
        n_past = k_past.shape[1]
        k_all = jnp.concatenate([k_past.astype(k.dtype), k], axis=1)
        v_all = jnp.concatenate([v_past.astype(v.dtype), v], axis=1)
        c_all = jnp.cumsum(jnp.concatenate([logf_past.astype(f32), logf], axis=1), axis=1)
        att = _fox_attend(q, c_all[:, n_past:], n_past + jnp.arange(T), k_all, v_all, c_all,
                          jnp.arange(n_past + T))
    xbc_c, conv_new = _causal_conv(xbc, conv_past, conv_w, conv_b)
    xs = xbc_c[..., :D_INNER]
    Bm = xbc_c[..., D_INNER:D_INNER + N_GROUPS * D_STATE].reshape(b, T, N_GROUPS, D_STATE)
    Cm = xbc_c[..., D_INNER + N_GROUPS * D_STATE:].reshape(b, T, N_GROUPS, D_STATE)
    dt = jax.nn.softplus(dt_raw.astype(f32) + dt_bias.astype(f32))
    A = -jnp.exp(A_log.astype(f32))
    xh = xs.reshape(b, T, N_HEADS_SSM, SSM_HEAD_DIM)
    y, h_new = _ssd(xh, dt, A, Bm, Cm, h0)
    y = y + xh.astype(f32) * D_skip.astype(f32)[:, None]
    y = y.reshape(b, T, D_INNER) * jax.nn.silu(z.astype(f32))
    yg = y.reshape(b, T, N_GROUPS, D_INNER // N_GROUPS)
    yg = yg * lax.rsqrt(jnp.mean(yg * yg, axis=-1, keepdims=True) + EPS)
    y_ssm = (yg.reshape(b, T, D_INNER) * ssm_norm_w.astype(f32)).astype(x.dtype)
    merged = (jax.nn.sigmoid(g_att) * (att.reshape(b, T, ATT_WIDTH) @ w_pa)
              + jax.nn.sigmoid(g_ssm) * (y_ssm @ w_pb))
    x = x + merged @ w_out
    h = _rmsnorm(x, norm2_w)
    x = x + jnp.square(jax.nn.relu(h @ w_up)) @ w_down
    new_state = (k, v, logf.astype(x.dtype), conv_new, h_new.astype(x.dtype))
    return x, new_state


def _stack(states, i):
    return jnp.stack([s[i] for s in states], axis=0)


def setup_inputs(seed: int = 0) -> dict:
    key = jax.random.key(seed)
    ks = jax.random.split(key, 32)
    f32 = jnp.float32

    def nrm(k, shape, scale):
        return jax.random.normal(k, shape, f32) * scale

    dt0 = jnp.exp(jax.random.uniform(ks[16], (DEPTH, N_HEADS_SSM), f32, np.log(1e-3), np.log(1e-1)))
    return {
        'x_prompt': nrm(ks[0], (BATCH, SEQ, D_MODEL), 1.0),
        'x_sample': nrm(ks[1], (DEC_BATCH, DEC_SEQ, D_MODEL), 1.0),
        'cache_k': nrm(ks[2], (DEPTH, DEC_BATCH, PAST_LEN, N_HEADS_ATT, HEAD_DIM), 1.0),
        'cache_v': nrm(ks[3], (DEPTH, DEC_BATCH, PAST_LEN, N_HEADS_ATT, HEAD_DIM), 1.0),
        'cache_logf': jax.nn.log_sigmoid(3.0 + nrm(ks[4], (DEPTH, DEC_BATCH, PAST_LEN, N_HEADS_ATT), 0.5)),
        'state_conv': nrm(ks[5], (DEPTH, DEC_BATCH, CONV_W - 1, CONV_DIM), 1.0),
        'state_ssm': nrm(ks[6], (DEPTH, DEC_BATCH, N_HEADS_SSM, SSM_HEAD_DIM, D_STATE), 0.1),
        'norm1_w': 1.0 + nrm(ks[7], (DEPTH, D_MODEL), 0.02),
        'w_in': nrm(ks[8], (DEPTH, D_MODEL, D_PROJ), D_MODEL ** -0.5),
        'b_f': 3.0 + nrm(ks[9], (DEPTH, N_HEADS_ATT), 0.3),
        'q_norm_w': 1.0 + nrm(ks[10], (DEPTH, HEAD_DIM), 0.02),
        'k_norm_w': 1.0 + nrm(ks[11], (DEPTH, HEAD_DIM), 0.02),
        'conv_w': nrm(ks[12], (DEPTH, CONV_W, CONV_DIM), 0.5),
        'conv_b': nrm(ks[13], (DEPTH, CONV_DIM), 0.02),
        'dt_bias': dt0 + jnp.log(-jnp.expm1(-dt0)),
        'A_log': jnp.log(jax.random.uniform(ks[14], (DEPTH, N_HEADS_SSM), f32, 1.0, 16.0)),
        'D_skip': 1.0 + nrm(ks[15], (DEPTH, N_HEADS_SSM), 0.1),
        'ssm_norm_w': 1.0 + nrm(ks[17], (DEPTH, D_INNER), 0.02),
        'w_pa': nrm(ks[18], (DEPTH, ATT_WIDTH, D_MODEL), ATT_WIDTH ** -0.5),
        'w_pb': nrm(ks[19], (DEPTH, D_INNER, D_MODEL), D_INNER ** -0.5),
        'w_out': nrm(ks[20], (DEPTH, D_MODEL, D_MODEL), D_MODEL ** -0.5),
        'norm2_w': 1.0 + nrm(ks[21], (DEPTH, D_MODEL), 0.02),
        'w_up': nrm(ks[22], (DEPTH, D_MODEL, D_FF), D_MODEL ** -0.5),
        'w_down': nrm(ks[23], (DEPTH, D_FF, D_MODEL), D_FF ** -0.5),
    }


def reference(x_prompt, x_sample, cache_k, cache_v, cache_logf, state_conv, state_ssm,
              norm1_w, w_in, b_f, q_norm_w, k_norm_w, conv_w, conv_b, dt_bias, A_log, D_skip,
              ssm_norm_w, w_pa, w_pb, w_out, norm2_w, w_up, w_down):
    yp = x_prompt
    ys = x_sample
    states_p = []
    states_s = []
    for l in range(DEPTH):
        params = (norm1_w[l], w_in[l], b_f[l], q_norm_w[l], k_norm_w[l], conv_w[l], conv_b[l],
                  dt_bias[l], A_log[l], D_skip[l], ssm_norm_w[l], w_pa[l], w_pb[l], w_out[l],
                  norm2_w[l], w_up[l], w_down[l])
        yp, st_p = _layer(yp, None, params)
        ys, st_s = _layer(ys, (cache_k[l], cache_v[l], cache_logf[l], state_conv[l], state_ssm[l]), params)
        states_p.append(st_p)
        states_s.append(st_s)
    k_prompt = _stack(states_p, 0)
    v_prompt = _stack(states_p, 1)
    logf_prompt = _stack(states_p, 2)
    conv_prompt = _stack(states_p, 3)
    ssm_prompt = _stack(states_p, 4)
    k_sample = _stack(states_s, 0)
    v_sample = _stack(states_s, 1)
    logf_sample = _stack(states_s, 2)
    conv_sample = _stack(states_s, 3)
    ssm_sample = _stack(states_s, 4)
    return (yp, ys, k_prompt, v_prompt, logf_prompt, conv_prompt, ssm_prompt,
            k_sample, v_sample, logf_sample, conv_sample, ssm_sample)
```

```python
import functools

import jax
import jax.numpy as jnp
from jax import lax
from jax.experimental import pallas as pl
from jax.experimental.pallas import tpu as pltpu

F32 = jnp.float32
BF16 = jnp.bfloat16

D_MODEL = 1024
N_HEADS_ATT = 16
HEAD_DIM = 64
ATT_WIDTH = N_HEADS_ATT * HEAD_DIM
ATT_SCALE = HEAD_DIM ** -0.5
D_INNER = 2048
SSM_HEAD_DIM = 64
N_HEADS_SSM = D_INNER // SSM_HEAD_DIM
N_GROUPS = 8
HEADS_PER_GROUP = N_HEADS_SSM // N_GROUPS
GROUP_WIDTH = D_INNER // N_GROUPS
D_STATE = 128
CONV_W = 4
CONV_DIM = D_INNER + 2 * N_GROUPS * D_STATE
D_FF = 4 * D_MODEL
EPS = 1e-6
SSD_CHUNK = 64

LANES = 128
SUBLANES = 8
NEG_BIG = -1e30
VMEM_LIMIT = 56 * 1024 * 1024

SMALL_F0 = 0
SMALL_DT0 = N_HEADS_ATT


def _split3(x):
    hi = x.astype(BF16)
    r1 = x - hi.astype(F32)
    mid = r1.astype(BF16)
    lo = (r1 - mid.astype(F32)).astype(BF16)
    return hi, mid, lo


def _dot(a, b):
    return jnp.dot(a, b, preferred_element_type=F32)


def _dot3(lhs_bf16, x):
    hi, mid, lo = _split3(x)
    return _dot(lhs_bf16, hi) + _dot(lhs_bf16, mid) + _dot(lhs_bf16, lo)


def _dot3_rhs(x, rhs_bf16):
    hi, mid, lo = _split3(x)
    return _dot(hi, rhs_bf16) + _dot(mid, rhs_bf16) + _dot(lo, rhs_bf16)


def _silu(x):
    return x * jax.nn.sigmoid(x)


def _softplus(x):
    return jnp.maximum(x, 0.0) + jnp.log1p(jnp.exp(-jnp.abs(x)))


def _log_sigmoid(x):
    return jnp.minimum(x, 0.0) - jnp.log1p(jnp.exp(-jnp.abs(x)))


def _lane_iota(shape):
    return lax.broadcasted_iota(jnp.int32, shape, len(shape) - 1)


def _row_iota(shape):
    return lax.broadcasted_iota(jnp.int32, shape, len(shape) - 2)


IN_TN = 1024
IN_QK_TILES = 2
IN_V_TILE = 2
IN_ZXG_TILE0 = 3
IN_MAIN_TILES = 11


def _inproj_kernel(x_ref, nw_ref, wm_ref, ws_ref, qk_ref, v_ref, zxg_ref, small_ref, xn_ref):
    j = pl.program_id(1)

    @pl.when(j == 0)
    def _():
        x = x_ref[...]
        ms = jnp.mean(x * x, axis=-1, keepdims=True)
        xn = (x * lax.rsqrt(ms + EPS)) * nw_ref[...]
        xn_ref[...] = xn.astype(BF16)
        small_ref[...] = _dot(xn_ref[...], ws_ref[...])

    acc = _dot(xn_ref[...], wm_ref[...])

    @pl.when(j < IN_QK_TILES)
    def _():
        qk_ref[...] = acc

    @pl.when(j == IN_V_TILE)
    def _():
        v_ref[...] = acc

    @pl.when(j >= IN_ZXG_TILE0)
    def _():
        zxg_ref[...] = acc


def _inproj(x2d, nw, w_main, w_small, tm):
    rows = x2d.shape[0]
    grid = (rows // tm, IN_MAIN_TILES)
    n_zxg = IN_MAIN_TILES - IN_ZXG_TILE0
    return pl.pallas_call(
        _inproj_kernel,
        grid=grid,
        in_specs=[
            pl.BlockSpec((tm, D_MODEL), lambda i, j: (i, 0)),
            pl.BlockSpec((1, D_MODEL), lambda i, j: (0, 0)),
            pl.BlockSpec((D_MODEL, IN_TN), lambda i, j: (0, j)),
            pl.BlockSpec((D_MODEL, LANES), lambda i, j: (0, 0)),
        ],
        out_specs=[
            pl.BlockSpec((tm, IN_TN), lambda i, j: (i, jnp.minimum(j, IN_QK_TILES - 1))),
            pl.BlockSpec((tm, IN_TN), lambda i, j: (i, 0)),
            pl.BlockSpec((tm, IN_TN), lambda i, j: (i, jnp.clip(j - IN_ZXG_TILE0, 0, n_zxg - 1))),
            pl.BlockSpec((tm, LANES), lambda i, j: (i, 0)),
        ],
        out_shape=[
            jax.ShapeDtypeStruct((rows, IN_QK_TILES * IN_TN), F32),
            jax.ShapeDtypeStruct((rows, IN_TN), F32),
            jax.ShapeDtypeStruct((rows, n_zxg * IN_TN), F32),
            jax.ShapeDtypeStruct((rows, LANES), F32),
        ],
        scratch_shapes=[pltpu.VMEM((tm, D_MODEL), BF16)],
        compiler_params=pltpu.CompilerParams(
            dimension_semantics=("arbitrary", "arbitrary"), vmem_limit_bytes=VMEM_LIMIT),
        name="inproj",
    )(x2d, nw, w_main, w_small)


AUG0 = HEAD_DIM


def _cum_logf(logf128, carry_ref, ltri_ref):
    tp = logf128.shape[0]
    c = carry_ref[0:1, :] + _dot3(ltri_ref[...], logf128)
    carry_ref[...] = jnp.broadcast_to(c[tp - 1:tp, :], carry_ref.shape)
    return c


def _c_parts(c):
    hi, mid, lo = _split3(c)
    lane = _lane_iota(c.shape)
    h = N_HEADS_ATT
    packed = jnp.where(
        lane < h, hi.astype(F32),
        jnp.where(lane < 2 * h, pltpu.roll(mid.astype(F32), h, 1),
                  jnp.where(lane < 3 * h, pltpu.roll(lo.astype(F32), 2 * h, 1),
                            jnp.where(lane == 3 * h, 1.0, 0.0))))
    return packed.astype(BF16)


def _head_rows(x, extras, out_ref, scale):
    lane = _lane_iota((x.shape[0], LANES))
    for hp in range(N_HEADS_ATT // 2):
        pair = x[:, LANES * hp:LANES * (hp + 1)] * scale
        for o in range(2):
            h = 2 * hp + o
            xh = pair if o == 0 else pltpu.roll(pair, HEAD_DIM, 1)
            ex = extras[:, LANES * h:LANES * (h + 1)]
            out_ref[0, h] = jnp.where(lane < HEAD_DIM, xh, ex).astype(BF16)


def _write_vt(v, vt_ref):
    vt = v.T
    for h in range(N_HEADS_ATT):
        vt_ref[0, h, 0] = vt[HEAD_DIM * h:HEAD_DIM * (h + 1), :].astype(BF16)


def _head_rmsnorm(x, g_ref, gt_ref, w):
    sq = x * x
    hi = sq.astype(BF16)
    lo = (sq - hi.astype(F32)).astype(BF16)
    ssq = _dot(hi, g_ref[...]) + _dot(lo, g_ref[...])
    rs = lax.rsqrt(ssq * (1.0 / HEAD_DIM) + EPS)
    rs_e = _dot3_rhs(rs, gt_ref[...])
    return (x * rs_e) * w


def _prep_new_kernel(q_ref, k_ref, v_ref, small_ref, c0_ref, bf_ref, qw_ref, kw_ref,
                     g_ref, gt_ref, selq_ref, selk_ref, ltri_ref, *rest, aliased):
    if aliased:
        rest = rest[2:]
    kout_ref, logf_ref, qaug_ref, kaug_ref, vt_ref, cend_ref, carry_ref = rest
    t = pl.program_id(1)

    @pl.when(t == 0)
    def _():
        carry_ref[...] = c0_ref[0]

    logf = _log_sigmoid(small_ref[...] + bf_ref[...])
    logf_ref[...] = logf[:, SMALL_F0:SMALL_F0 + N_HEADS_ATT]
    c = _cum_logf(logf, carry_ref, ltri_ref)
    cend_ref[0] = carry_ref[...]
    cp = _c_parts(c)

    qn = _head_rmsnorm(q_ref[...], g_ref, gt_ref, qw_ref[...])
    _head_rows(qn, _dot(cp, selq_ref[...]), qaug_ref, ATT_SCALE)
    kn = _head_rmsnorm(k_ref[...], g_ref, gt_ref, kw_ref[...])
    kout_ref[...] = kn
    _head_rows(kn, _dot(cp, selk_ref[...]), kaug_ref, 1.0)
    _write_vt(v_ref[...], vt_ref)


def _prep_cache_kernel(k_ref, v_ref, logf_ref, c0_ref, selk_ref, ltri_ref,
                       kaug_ref, vt_ref, cend_ref, carry_ref):
    t = pl.program_id(1)

    @pl.when(t == 0)
    def _():
        carry_ref[...] = c0_ref[0]

    c = _cum_logf(logf_ref[0], carry_ref, ltri_ref)
    cend_ref[0] = carry_ref[...]
    _head_rows(k_ref[0], _dot(_c_parts(c), selk_ref[...]), kaug_ref, 1.0)
    _write_vt(v_ref[0], vt_ref)


def _aug_selectors():
    h = N_HEADS_ATT
    rows = jnp.arange(LANES)[:, None]
    cols = jnp.arange(h * LANES)[None, :]
    head = cols // LANES
    pos = cols % LANES
    one_row = rows == 3 * h
    selq = jnp.zeros((LANES, h * LANES), F32)
    selk = jnp.zeros((LANES, h * LANES), F32)
    for part in range(3):
        part_row = rows == part * h + head
        selq = selq + jnp.where(part_row & (pos == AUG0 + part), 1.0, 0.0)
        selq = selq + jnp.where(one_row & (pos == AUG0 + 3 + part), 1.0, 0.0)
        selk = selk + jnp.where(one_row & (pos == AUG0 + part), 1.0, 0.0)
        selk = selk - jnp.where(part_row & (pos == AUG0 + 3 + part), 1.0, 0.0)
    return selq.astype(BF16), selk.astype(BF16)


def _head_group_matrices():
    j = jnp.arange(ATT_WIDTH)[:, None] // HEAD_DIM
    h = jnp.arange(LANES)[None, :]
    g = jnp.where(j == h, 1.0, 0.0).astype(BF16)
    return g, g.T


def _ltri(n, block=None):
    r = jnp.arange(n)[:, None]
    c = jnp.arange(n)[None, :]
    keep = c <= r
    if block is not None:
        keep = keep & (r // block == c // block)
    return jnp.where(keep, 1.0, 0.0).astype(BF16)


def _prep_new(qk, v, small, c0, bf128, qw, kw, batch, seq, tk, bufs=None, t_off_blocks=0):
    nt = seq // tk
    h = N_HEADS_ATT
    g, gt = _head_group_matrices()
    selq, selk = _aug_selectors()
    ltri = _ltri(tk)
    rows = batch * seq
    const = lambda shape: pl.BlockSpec(shape, lambda b, t: tuple(0 for _ in shape))
    in_specs = [
        pl.BlockSpec((tk, ATT_WIDTH), lambda b, t: (b * nt + t, 0)),
        pl.BlockSpec((tk, ATT_WIDTH), lambda b, t: (b * nt + t, 1)),
        pl.BlockSpec((tk, ATT_WIDTH), lambda b, t: (b * nt + t, 0)),
        pl.BlockSpec((tk, LANES), lambda b, t: (b * nt + t, 0)),
        pl.BlockSpec((1, SUBLANES, LANES), lambda b, t: (b, 0, 0)),
        const((1, LANES)), const((1, ATT_WIDTH)), const((1, ATT_WIDTH)),
        const((ATT_WIDTH, LANES)), const((LANES, ATT_WIDTH)),
        const((LANES, h * LANES)), const((LANES, h * LANES)), const((tk, tk)),
    ]
    args = [qk, qk, v, small, c0, bf128, qw, kw, g, gt, selq, selk, ltri]
    if bufs is None:
        kv_len = seq
        aliases = {}
    else:
        kv_len = bufs[0].shape[2]
        in_specs += [pl.BlockSpec(memory_space=pl.ANY), pl.BlockSpec(memory_space=pl.ANY)]
        aliases = {len(args): 3, len(args) + 1: 4}
        args += list(bufs)
    o = t_off_blocks
    out_specs = [
        pl.BlockSpec((tk, ATT_WIDTH), lambda b, t: (b * nt + t, 0)),
        pl.BlockSpec((tk, h), lambda b, t: (b * nt + t, 0)),
        pl.BlockSpec((1, h, tk, LANES), lambda b, t: (b, 0, t, 0)),
        pl.BlockSpec((1, h, tk, LANES), lambda b, t: (b, 0, t + o, 0)),
        pl.BlockSpec((1, h, 1, HEAD_DIM, tk), lambda b, t: (b, 0, t + o, 0, 0)),
        pl.BlockSpec((1, SUBLANES, LANES), lambda b, t: (b, 0, 0)),
    ]
    out_shape = [
        jax.ShapeDtypeStruct((rows, ATT_WIDTH), F32),
        jax.ShapeDtypeStruct((rows, h), F32),
        jax.ShapeDtypeStruct((batch, h, seq, LANES), BF16),
        jax.ShapeDtypeStruct((batch, h, kv_len, LANES), BF16),
        jax.ShapeDtypeStruct((batch, h, kv_len // tk, HEAD_DIM, tk), BF16),
        jax.ShapeDtypeStruct((batch, SUBLANES, LANES), F32),
    ]
    return pl.pallas_call(
        functools.partial(_prep_new_kernel, aliased=bufs is not None),
        grid=(batch, nt),
        in_specs=in_specs,
        out_specs=out_specs,
        out_shape=out_shape,
        scratch_shapes=[pltpu.VMEM((SUBLANES, LANES), F32)],
        input_output_aliases=aliases,
        compiler_params=pltpu.CompilerParams(
            dimension_semantics=("arbitrary", "arbitrary"), vmem_limit_bytes=VMEM_LIMIT),
        name="prep_new",
    )(*args)


def _prep_cache(cache_k, cache_v, cache_logf128, c0, tk, kv_len):
    batch, past, _ = cache_k.shape
    nt = past // tk
    h = N_HEADS_ATT
    _, selk = _aug_selectors()
    ltri = _ltri(tk)
    const = lambda shape: pl.BlockSpec(shape, lambda b, t: tuple(0 for _ in shape))
    return pl.pallas_call(
        _prep_cache_kernel,
        grid=(batch, nt),
        in_specs=[
            pl.BlockSpec((1, tk, ATT_WIDTH), lambda b, t: (b, t, 0)),
            pl.BlockSpec((1, tk, ATT_WIDTH), lambda b, t: (b, t, 0)),
            pl.BlockSpec((1, tk, LANES), lambda b, t: (b, t, 0)),
            pl.BlockSpec((1, SUBLANES, LANES), lambda b, t: (b, 0, 0)),
            const((LANES, h * LANES)), const((tk, tk)),
        ],
        out_specs=[
            pl.BlockSpec((1, h, tk, LANES), lambda b, t: (b, 0, t, 0)),
            pl.BlockSpec((1, h, 1, HEAD_DIM, tk), lambda b, t: (b, 0, t, 0, 0)),
            pl.BlockSpec((1, SUBLANES, LANES), lambda b, t: (b, 0, 0)),
        ],
        out_shape=[
            jax.ShapeDtypeStruct((batch, h, kv_len, LANES), BF16),
            jax.ShapeDtypeStruct((batch, h, kv_len // tk, HEAD_DIM, tk), BF16),
            jax.ShapeDtypeStruct((batch, SUBLANES, LANES), F32),
        ],
        scratch_shapes=[pltpu.VMEM((SUBLANES, LANES), F32)],
        compiler_params=pltpu.CompilerParams(
            dimension_semantics=("arbitrary", "arbitrary"), vmem_limit_bytes=VMEM_LIMIT),
        name="prep_cache",
    )(cache_k, cache_v, cache_logf128, c0, selk, ltri)


def _flash_kernel(q_ref, k_ref, vt_ref, o_ref, m_ref, l_ref, acc_ref, *, tk, q_off_blocks):
    qi = pl.program_id(2)
    n_full = q_off_blocks + qi
    m_ref[...] = jnp.full(m_ref.shape, NEG_BIG, F32)
    l_ref[...] = jnp.zeros(l_ref.shape, F32)
    acc_ref[...] = jnp.zeros(acc_ref.shape, F32)

    def step(kj, masked):
        for o in range(2):
            kc = k_ref[0, o, pl.ds(pl.multiple_of(kj * tk, tk), tk), :]
            st = lax.dot_general(kc, q_ref[0, o], (((1,), (1,)), ((), ())),
                                 preferred_element_type=F32)
            if masked:
                st = jnp.where(_row_iota(st.shape) <= _lane_iota(st.shape), st, NEG_BIG)
            m_prev = m_ref[o]
            m_new = jnp.maximum(m_prev, jnp.max(st, axis=0, keepdims=True))
            alpha = jnp.exp(m_prev - m_new)
            p = jnp.exp(st - m_new)
            l_ref[o] = alpha * l_ref[o] + jnp.sum(p, axis=0, keepdims=True)
            acc_ref[o] = alpha * acc_ref[o] + _dot(vt_ref[0, o, kj], p.astype(BF16))
            m_ref[o] = m_new

    def body(kj, carry):
        step(kj, False)
        return carry

    lax.fori_loop(0, n_full, body, 0)
    step(n_full, True)
    out = jnp.concatenate([acc_ref[0] / l_ref[0], acc_ref[1] / l_ref[1]], axis=0)
    o_ref[0] = out.T.astype(BF16)


def _flash(q_aug, k_aug, vt, tk, q_off_blocks):
    batch, h, tq_total, _ = q_aug.shape
    kv_len = k_aug.shape[2]
    nq = tq_total // tk
    return pl.pallas_call(
        functools.partial(_flash_kernel, tk=tk, q_off_blocks=q_off_blocks),
        grid=(batch, h // 2, nq),
        in_specs=[
            pl.BlockSpec((1, 2, tk, LANES), lambda b, hp, qi: (b, hp, qi, 0)),
            pl.BlockSpec((1, 2, kv_len, LANES), lambda b, hp, qi: (b, hp, 0, 0)),
            pl.BlockSpec((1, 2, kv_len // tk, HEAD_DIM, tk), lambda b, hp, qi: (b, hp, 0, 0, 0)),
        ],
        out_specs=pl.BlockSpec((1, tk, LANES), lambda b, hp, qi: (b, qi, hp)),
        out_shape=jax.ShapeDtypeStruct((batch, tq_total, ATT_WIDTH), BF16),
        scratch_shapes=[
            pltpu.VMEM((2, 1, tk), F32),
            pltpu.VMEM((2, 1, tk), F32),
            pltpu.VMEM((2, HEAD_DIM, tk), F32),
        ],
        compiler_params=pltpu.CompilerParams(
            dimension_semantics=("arbitrary", "arbitrary", "arbitrary"),
            vmem_limit_bytes=VMEM_LIMIT),
        name="flash",
    )(q_aug, k_aug, vt)


B_LANE0 = D_INNER
C_LANE0 = D_INNER + N_GROUPS * D_STATE
PK_HI0 = SMALL_DT0
PK_MID0 = PK_HI0 + N_HEADS_SSM
PK_LO0 = PK_MID0 + N_HEADS_SSM


def _pack_head_parts(x):
    hi, mid, lo = _split3(x)
    lane = _lane_iota(x.shape)
    packed = jnp.where(
        lane < PK_MID0, hi.astype(F32),
        jnp.where(lane < PK_LO0, pltpu.roll(mid.astype(F32), N_HEADS_SSM, 1),
                  pltpu.roll(lo.astype(F32), 2 * N_HEADS_SSM, 1)))
    return packed.astype(BF16)


def _head_expand_matrix():
    rows = jnp.arange(LANES)[:, None]
    head = jnp.arange(D_INNER)[None, :] // SSM_HEAD_DIM
    hit = (rows == PK_HI0 + head) | (rows == PK_MID0 + head) | (rows == PK_LO0 + head)
    return jnp.where(hit, 1.0, 0.0).astype(BF16)


def _ssd_kernel(z_ref, xs_ref, bc_ref, small_ref, cpast_ref, h0_ref, cw_ref, cb_ref,
                dtb_ref, alog_ref, dskip_ref, nw_ref, lblk_ref, sel_ref,
                y_ref, cnew_ref, hlast_ref,
                xbuf_ref, ht_ref, ae_ref, dte_ref, *, tb, n_valid, tail_row):
    t = pl.program_id(1)
    nt = pl.num_programs(1)
    n_state_blocks = D_INNER // LANES

    @pl.when(t == 0)
    def _():
        xbuf_ref[0:SUBLANES, :] = cpast_ref[0]
        for i in range(n_state_blocks):
            ht_ref[:, LANES * i:LANES * (i + 1)] = h0_ref[0, LANES * i:LANES * (i + 1), :].T

    xbuf_ref[SUBLANES:SUBLANES + tb, 0:D_INNER] = xs_ref[...]
    xbuf_ref[SUBLANES:SUBLANES + tb, D_INNER:CONV_DIM] = bc_ref[...]

    lane = _lane_iota((tb, LANES))
    row = _row_iota((tb, LANES)) + t * tb
    live = (lane >= SMALL_DT0) & (lane < SMALL_DT0 + N_HEADS_SSM) & (row < n_valid)
    dt = jnp.where(live, _softplus(small_ref[...] + dtb_ref[...]), 0.0)
    a = dt * (-jnp.exp(alog_ref[...]))
    acum = _dot3(lblk_ref[...], a)
    ae_ref[...] = _dot(_pack_head_parts(acum), sel_ref[...])
    dte_ref[...] = _dot(_pack_head_parts(dt), sel_ref[...])

    gw = GROUP_WIDTH
    sub = _row_iota((SSD_CHUNK, gw))
    pos = _lane_iota((SSD_CHUNK, gw)) % SSM_HEAD_DIM
    diag = pos == sub
    causal = pos <= sub
    blk_row = _row_iota((gw, gw)) // SSM_HEAD_DIM
    blk_col = _lane_iota((gw, gw)) // SSM_HEAD_DIM
    same_head = blk_row == blk_col

    def conv(c0, lane0, width):
        blk = xbuf_ref[pl.ds(c0, SSD_CHUNK + SUBLANES), lane0:lane0 + width]
        out = cb_ref[:, lane0:lane0 + width]
        for i in range(CONV_W):
            shifted = blk if i == CONV_W - 1 else pltpu.roll(blk, CONV_W - 1 - i, 0)
            out = out + shifted[SUBLANES:, :] * cw_ref[i:i + 1, lane0:lane0 + width]
        return _silu(out)

    def chunk(c, carry):
        c0 = pl.multiple_of(c * SSD_CHUNK, SSD_CHUNK)
        rows = pl.ds(c0, SSD_CHUNK)
        for g in range(N_GROUPS):
            lanes = slice(gw * g, gw * (g + 1))
            xs = conv(c0, gw * g, gw)
            bm = conv(c0, B_LANE0 + D_STATE * g, D_STATE)
            cm = conv(c0, C_LANE0 + D_STATE * g, D_STATE)
            ae = ae_ref[rows, lanes]
            xdt = xs * dte_ref[rows, lanes]
            bm16 = bm.astype(BF16)
            cm16 = cm.astype(BF16)

            b4 = jnp.concatenate([bm16] * HEADS_PER_GROUP, axis=0)
            cb4 = lax.dot_general(cm16, b4, (((1,), (1,)), ((), ())), preferred_element_type=F32)
            a_row = jnp.sum(jnp.where(diag, ae, 0.0), axis=0, keepdims=True)
            decay = jnp.exp(jnp.where(causal, ae - a_row, NEG_BIG))
            xdt16 = xdt.astype(BF16)
            x4 = jnp.where(same_head, jnp.concatenate([xdt16] * HEADS_PER_GROUP, axis=0),
                           jnp.zeros((), BF16))
            y = _dot((cb4 * decay).astype(BF16), x4)

            hprev = ht_ref[:, lanes]
            y = y + jnp.exp(ae) * _dot(cm16, hprev.astype(BF16))
            a_last = ae[SSD_CHUNK - 1:SSD_CHUNK, :]
            xw = (xdt * jnp.exp(a_last - ae)).astype(BF16)
            upd = lax.dot_general(bm16, xw, (((0,), (0,)), ((), ())), preferred_element_type=F32)
            ht_ref[:, lanes] = hprev * jnp.exp(a_last) + upd

            y = y + xs * dskip_ref[:, lanes]
            y = y * _silu(z_ref[rows, lanes])
            ms = jnp.mean(y * y, axis=-1, keepdims=True)
            y_ref[rows, lanes] = ((y * lax.rsqrt(ms + EPS)) * nw_ref[:, lanes]).astype(BF16)
        return carry

    lax.fori_loop(0, tb // SSD_CHUNK, chunk, 0)

    cnew_ref[0] = xbuf_ref[tail_row:tail_row + SUBLANES, :]
    xbuf_ref[0:SUBLANES, :] = xbuf_ref[tb:tb + SUBLANES, :]

    @pl.when(t == nt - 1)
    def _():
        for i in range(n_state_blocks):
            hlast_ref[0, LANES * i:LANES * (i + 1), :] = ht_ref[:, LANES * i:LANES * (i + 1)].T


def _ssd(zxg, small, conv_past8, h0, cw, cb, dtb128, alog128, dskip_e, nw, batch, seq, tb, n_valid):
    nt = seq // tb
    rows = batch * seq
    tail_row = n_valid - (seq - tb)
    lblk = _ltri(tb, SSD_CHUNK)
    sel = _head_expand_matrix()
    const = lambda shape: pl.BlockSpec(shape, lambda b, t: tuple(0 for _ in shape))
    return pl.pallas_call(
        functools.partial(_ssd_kernel, tb=tb, n_valid=n_valid, tail_row=tail_row),
        grid=(batch, nt),
        in_specs=[
            pl.BlockSpec((tb, D_INNER), lambda b, t: (b * nt + t, 0)),
            pl.BlockSpec((tb, D_INNER), lambda b, t: (b * nt + t, 1)),
            pl.BlockSpec((tb, D_INNER), lambda b, t: (b * nt + t, 2)),
            pl.BlockSpec((tb, LANES), lambda b, t: (b * nt + t, 0)),
            pl.BlockSpec((1, SUBLANES, CONV_DIM), lambda b, t: (b, 0, 0)),
            pl.BlockSpec((1, D_INNER, D_STATE), lambda b, t: (b, 0, 0)),
            const((CONV_W, CONV_DIM)), const((1, CONV_DIM)),
            const((1, LANES)), const((1, LANES)), const((1, D_INNER)), const((1, D_INNER)),
            const((tb, tb)), const((LANES, D_INNER)),
        ],
        out_specs=[
            pl.BlockSpec((tb, D_INNER), lambda b, t: (b * nt + t, 0)),
            pl.BlockSpec((1, SUBLANES, CONV_DIM), lambda b, t: (b, 0, 0)),
            pl.BlockSpec((1, D_INNER, D_STATE), lambda b, t: (b, 0, 0)),
        ],
        out_shape=[
            jax.ShapeDtypeStruct((rows, D_INNER), BF16),
            jax.ShapeDtypeStruct((batch, SUBLANES, CONV_DIM), F32),
            jax.ShapeDtypeStruct((batch, D_INNER, D_STATE), F32),
        ],
        scratch_shapes=[
            pltpu.VMEM((tb + SUBLANES, CONV_DIM), F32),
            pltpu.VMEM((D_STATE, D_INNER), F32),
            pltpu.VMEM((tb, D_INNER), F32),
            pltpu.VMEM((tb, D_INNER), F32),
        ],
        compiler_params=pltpu.CompilerParams(
            dimension_semantics=("arbitrary", "arbitrary"), vmem_limit_bytes=VMEM_LIMIT),
        name="ssd",
    )(zxg, zxg, zxg, small, conv_past8, h0, cw, cb, dtb128, alog128, dskip_e, nw, lblk, sel)


FF_CHUNK = 1024


def _merge_kernel(att_ref, ys_ref, ga_ref, gs_ref, x_ref, wpa_ref, wpb_ref, wout_ref,
                  n2_ref, wup_ref, wdn_ref, o_ref):
    merged = (jax.nn.sigmoid(ga_ref[...]) * _dot(att_ref[...], wpa_ref[...])
              + jax.nn.sigmoid(gs_ref[...]) * _dot(ys_ref[...], wpb_ref[...]))
    x1 = x_ref[...] + _dot(merged.astype(BF16), wout_ref[...])
    ms = jnp.mean(x1 * x1, axis=-1, keepdims=True)
    hn = ((x1 * lax.rsqrt(ms + EPS)) * n2_ref[...]).astype(BF16)
    acc = x1
    for c in range(D_FF // FF_CHUNK):
        cols = slice(FF_CHUNK * c, FF_CHUNK * (c + 1))
        up = jnp.maximum(_dot(hn, wup_ref[:, cols]), 0.0)
        acc = acc + _dot((up * up).astype(BF16), wdn_ref[cols, :])
    o_ref[...] = acc


def _merge(att, ys, zxg, x2d, wpa, wpb, wout, n2, wup, wdn, tm):
    rows = x2d.shape[0]
    ga_blk = 6
    gs_blk = 7
    resident = lambda shape: pl.BlockSpec(shape, lambda i: tuple(0 for _ in shape),
                                          pipeline_mode=pl.Buffered(1))
    return pl.pallas_call(
        _merge_kernel,
        grid=(rows // tm,),
        in_specs=[
            pl.BlockSpec((tm, ATT_WIDTH), lambda i: (i, 0)),
            pl.BlockSpec((tm, D_INNER), lambda i: (i, 0)),
            pl.BlockSpec((tm, D_MODEL), lambda i: (i, ga_blk)),
            pl.BlockSpec((tm, D_MODEL), lambda i: (i, gs_blk)),
            pl.BlockSpec((tm, D_MODEL), lambda i: (i, 0)),
            resident((ATT_WIDTH, D_MODEL)), resident((D_INNER, D_MODEL)),
            resident((D_MODEL, D_MODEL)), resident((1, D_MODEL)),
            resident((D_MODEL, D_FF)), resident((D_FF, D_MODEL)),
        ],
        out_specs=pl.BlockSpec((tm, D_MODEL), lambda i: (i, 0)),
        out_shape=jax.ShapeDtypeStruct((rows, D_MODEL), F32),
        compiler_params=pltpu.CompilerParams(
            dimension_semantics=("arbitrary",), vmem_limit_bytes=VMEM_LIMIT),
        name="merge",
    )(att, ys, zxg, zxg, x2d, wpa, wpb, wout, n2, wup, wdn)


def _pick_tile(n, prefs):
    for p in prefs:
        if n % p == 0:
            return p
    raise ValueError(f"no tile in {prefs} divides {n}")


def _pad_lanes(vec, lane0):
    return jnp.zeros((1, LANES), F32).at[0, lane0:lane0 + vec.shape[0]].set(vec.astype(F32))


def _layer(x, past, p, n_valid):
    batch, seq, _ = x.shape
    rows = batch * seq
    x2d = x.reshape(rows, D_MODEL)

    tm = _pick_tile(rows, (1024, 512, 256, 128))
    qk, v, zxg, small = _inproj(x2d, p["norm1_w"], p["w_main"], p["w_small"], tm)

    tk = _pick_tile(seq, (512, 256, 128)) if past is None else LANES
    zeros_c = jnp.zeros((batch, SUBLANES, LANES), F32)
    if past is None:
        k_out, logf, q_aug, k_aug, vt, _ = _prep_new(
            qk, v, small, zeros_c, p["bf128"], p["qw"], p["kw"], batch, seq, tk)
        q_off_blocks = 0
        conv_past8 = jnp.zeros((batch, SUBLANES, CONV_DIM), F32)
        h0 = jnp.zeros((batch, D_INNER, D_STATE), F32)
    else:
        cache_k, cache_v, cache_logf, conv_past, h0 = past
        n_past = cache_k.shape[1]
        kv_len = n_past + seq
        logf128 = jnp.pad(cache_logf.astype(F32), ((0, 0), (0, 0), (0, LANES - N_HEADS_ATT)))
        k_buf, vt_buf, c_past = _prep_cache(cache_k, cache_v, logf128, zeros_c, tk, kv_len)
        k_out, logf, q_aug, k_aug, vt, _ = _prep_new(
            qk, v, small, c_past, p["bf128"], p["qw"], p["kw"], batch, seq, tk,
            bufs=(k_buf, vt_buf), t_off_blocks=n_past // tk)
        q_off_blocks = n_past // tk
        conv_past8 = jnp.pad(conv_past.astype(F32), ((0, 0), (SUBLANES - (CONV_W - 1), 0), (0, 0)))
        h0 = h0.astype(F32).reshape(batch, D_INNER, D_STATE)

    att = _flash(q_aug, k_aug, vt, tk, q_off_blocks)

    tb = _pick_tile(seq, (256, 128))
    ys, conv_new8, h_last = _ssd(zxg, small, conv_past8, h0, p["conv_w"], p["conv_b"],
                                 p["dtb128"], p["alog128"], p["dskip_e"], p["ssm_norm_w"],
                                 batch, seq, tb, n_valid)

    tmm = _pick_tile(rows, (512, 256, 128))
    y = _merge(att.reshape(rows, ATT_WIDTH), ys, zxg, x2d, p["w_pa"], p["w_pb"], p["w_out"],
               p["norm2_w"], p["w_up"], p["w_down"], tmm)

    nv = n_valid
    y = y.reshape(batch, seq, D_MODEL)[:, :nv]
    k_new = k_out.reshape(batch, seq, N_HEADS_ATT, HEAD_DIM)[:, :nv]
    v_new = v.reshape(batch, seq, N_HEADS_ATT, HEAD_DIM)[:, :nv]
    logf_new = logf.reshape(batch, seq, N_HEADS_ATT)[:, :nv]
    conv_new = conv_new8[:, SUBLANES - (CONV_W - 1):, :]
    h_new = h_last.reshape(batch, N_HEADS_SSM, SSM_HEAD_DIM, D_STATE)
    return y, (k_new, v_new, logf_new, conv_new, h_new)


def _layer_params(l, norm1_w, w_in, b_f, q_norm_w, k_norm_w, conv_w, conv_b, dt_bias, A_log,
                  D_skip, ssm_norm_w, w_pa, w_pb, w_out, norm2_w, w_up, w_down):
    w = w_in[l]
    o = 0
    seg = {}
    for name, size in (("q", ATT_WIDTH), ("k", ATT_WIDTH), ("v", ATT_WIDTH), ("f", N_HEADS_ATT),
                       ("z", D_INNER), ("xbc", CONV_DIM), ("dt", N_HEADS_SSM),
                       ("ga", D_MODEL), ("gs", D_MODEL)):
        seg[name] = w[:, o:o + size]
        o += size
    w_main = jnp.concatenate([seg[n] for n in ("q", "k", "v", "z", "xbc", "ga", "gs")], axis=1)
    w_small = jnp.zeros((D_MODEL, LANES), w.dtype)
    w_small = w_small.at[:, SMALL_F0:SMALL_F0 + N_HEADS_ATT].set(seg["f"])
    w_small = w_small.at[:, SMALL_DT0:SMALL_DT0 + N_HEADS_SSM].set(seg["dt"])
    row = lambda a: a.astype(F32).reshape(1, -1)
    return {
        "norm1_w": row(norm1_w[l]),
        "w_main": w_main.astype(BF16),
        "w_small": w_small.astype(BF16),
        "bf128": _pad_lanes(b_f[l], SMALL_F0),
        "qw": row(jnp.tile(q_norm_w[l], N_HEADS_ATT)),
        "kw": row(jnp.tile(k_norm_w[l], N_HEADS_ATT)),
        "conv_w": conv_w[l].astype(F32),
        "conv_b": row(conv_b[l]),
        "dtb128": _pad_lanes(dt_bias[l], SMALL_DT0),
        "alog128": _pad_lanes(A_log[l], SMALL_DT0),
        "dskip_e": row(jnp.repeat(D_skip[l], SSM_HEAD_DIM)),
        "ssm_norm_w": row(ssm_norm_w[l]),
        "w_pa": w_pa[l].astype(BF16),
        "w_pb": w_pb[l].astype(BF16),
        "w_out": w_out[l].astype(BF16),
        "norm2_w": row(norm2_w[l]),
        "w_up": w_up[l].astype(BF16),
        "w_down": w_down[l].astype(BF16),
    }


def kernel(x_prompt, x_sample, cache_k, cache_v, cache_logf, state_conv, state_ssm, norm1_w, w_in,
           b_f, q_norm_w, k_norm_w, conv_w, conv_b, dt_bias, A_log, D_skip, ssm_norm_w, w_pa, w_pb,
           w_out, norm2_w, w_up, w_down):
    depth = w_in.shape[0]
    dec_batch, dec_seq, _ = x_sample.shape
    dec_pad = -(-dec_seq // LANES) * LANES
    yp = x_prompt
    ys = jnp.pad(x_sample, ((0, 0), (0, dec_pad - dec_seq), (0, 0)))
    states_p, states_s = [], []
    for l in range(depth):
        p = _layer_params(l, norm1_w, w_in, b_f, q_norm_w, k_norm_w, conv_w, conv_b, dt_bias,
                          A_log, D_skip, ssm_norm_w, w_pa, w_pb, w_out, norm2_w, w_up, w_down)
        yp, st_p = _layer(yp, None, p, yp.shape[1])
        n_past = cache_k.shape[2]
        past = (cache_k[l].reshape(dec_batch, n_past, ATT_WIDTH),
                cache_v[l].reshape(dec_batch, n_past, ATT_WIDTH),
                cache_logf[l], state_conv[l], state_ssm[l])
        ys_l, st_s = _layer(ys, past, p, dec_seq)
        states_p.append(st_p)
        states_s.append(st_s)
        if l + 1 < depth:
            ys = jnp.pad(ys_l, ((0, 0), (0, dec_pad - dec_seq), (0, 0)))
        else:
            ys = ys_l
    stack = lambda states, i: jnp.stack([s[i] for s in states], axis=0)
    return (yp, ys,
            stack(states_p, 0), stack(states_p, 1), stack(states_p, 2), stack(states_p, 3),
            stack(states_p, 4),
            stack(states_s, 0), stack(states_s, 1), stack(states_s, 2), stack(states_s, 3),
            stack(states_s, 4))
```

```python
import functools

import jax
import jax.numpy as jnp
from jax import lax
from jax.experimental import pallas as pl
from jax.experimental.pallas import tpu as pltpu

F32 = jnp.float32
BF16 = jnp.bfloat16

D_MODEL = 1024
N_HEADS_ATT = 16
HEAD_DIM = 64
ATT_WIDTH = N_HEADS_ATT * HEAD_DIM
ATT_SCALE = HEAD_DIM ** -0.5
D_INNER = 2048
SSM_HEAD_DIM = 64
N_HEADS_SSM = D_INNER // SSM_HEAD_DIM
N_GROUPS = 8
HEADS_PER_GROUP = N_HEADS_SSM // N_GROUPS
GROUP_WIDTH = D_INNER // N_GROUPS
D_STATE = 128
CONV_W = 4
CONV_DIM = D_INNER + 2 * N_GROUPS * D_STATE
D_FF = 4 * D_MODEL
EPS = 1e-6
SSD_CHUNK = 64

LANES = 128
SUBLANES = 8
NEG_BIG = -1e30
LOG2E = 1.4426950408889634
VMEM_LIMIT = 56 * 1024 * 1024

SMALL_F0 = 0
SMALL_DT0 = N_HEADS_ATT


def _split3(x):
    hi = x.astype(BF16)
    r1 = x - hi.astype(F32)
    mid = r1.astype(BF16)
    lo = (r1 - mid.astype(F32)).astype(BF16)
    return hi, mid, lo


def _dot(a, b):
    return jnp.dot(a, b, preferred_element_type=F32)


def _dot3(lhs_bf16, x):
    hi, mid, lo = _split3(x)
    return _dot(lhs_bf16, hi) + _dot(lhs_bf16, mid) + _dot(lhs_bf16, lo)


def _dot3_rhs(x, rhs_bf16):
    hi, mid, lo = _split3(x)
    return _dot(hi, rhs_bf16) + _dot(mid, rhs_bf16) + _dot(lo, rhs_bf16)


def _sigmoid(x):
    return 0.5 * jnp.tanh(0.5 * x) + 0.5


def _silu(x):
    h = 0.5 * x
    return h + h * jnp.tanh(h)


def _softplus(x):
    return jnp.maximum(x, 0.0) + jnp.log1p(jnp.exp(-jnp.abs(x)))


def _log_sigmoid(x):
    return jnp.minimum(x, 0.0) - jnp.log1p(jnp.exp(-jnp.abs(x)))


def _lane_iota(shape):
    return lax.broadcasted_iota(jnp.int32, shape, len(shape) - 1)


def _row_iota(shape):
    return lax.broadcasted_iota(jnp.int32, shape, len(shape) - 2)


IN_TN = 1024
IN_QK_TILES = 2
IN_V_TILE = 2
IN_ZXG_TILE0 = 3
IN_MAIN_TILES = 11


def _inproj_kernel(x_ref, nw_ref, wm_ref, ws_ref, qk_ref, v_ref, zxg_ref, small_ref, xn_ref):
    j = pl.program_id(1)

    @pl.when(j == 0)
    def _():
        x = x_ref[...]
        ms = jnp.mean(x * x, axis=-1, keepdims=True)
        xn = (x * lax.rsqrt(ms + EPS)) * nw_ref[...]
        xn_ref[...] = xn.astype(BF16)
        small_ref[...] = _dot(xn_ref[...], ws_ref[...])

    acc = _dot(xn_ref[...], wm_ref[...])

    @pl.when(j < IN_QK_TILES)
    def _():
        qk_ref[...] = acc

    @pl.when(j == IN_V_TILE)
    def _():
        v_ref[...] = acc

    @pl.when(j >= IN_ZXG_TILE0)
    def _():
        zxg_ref[...] = acc


def _inproj(x2d, nw, w_main, w_small, tm):
    rows = x2d.shape[0]
    grid = (rows // tm, IN_MAIN_TILES)
    n_zxg = IN_MAIN_TILES - IN_ZXG_TILE0
    return pl.pallas_call(
        _inproj_kernel,
        grid=grid,
        in_specs=[
            pl.BlockSpec((tm, D_MODEL), lambda i, j: (i, 0)),
            pl.BlockSpec((1, D_MODEL), lambda i, j: (0, 0)),
            pl.BlockSpec((D_MODEL, IN_TN), lambda i, j: (0, j)),
            pl.BlockSpec((D_MODEL, LANES), lambda i, j: (0, 0)),
        ],
        out_specs=[
            pl.BlockSpec((tm, IN_TN), lambda i, j: (i, jnp.minimum(j, IN_QK_TILES - 1))),
            pl.BlockSpec((tm, IN_TN), lambda i, j: (i, 0)),
            pl.BlockSpec((tm, IN_TN), lambda i, j: (i, jnp.clip(j - IN_ZXG_TILE0, 0, n_zxg - 1))),
            pl.BlockSpec((tm, LANES), lambda i, j: (i, 0)),
        ],
        out_shape=[
            jax.ShapeDtypeStruct((rows, IN_QK_TILES * IN_TN), F32),
            jax.ShapeDtypeStruct((rows, IN_TN), F32),
            jax.ShapeDtypeStruct((rows, n_zxg * IN_TN), F32),
            jax.ShapeDtypeStruct((rows, LANES), F32),
        ],
        scratch_shapes=[pltpu.VMEM((tm, D_MODEL), BF16)],
        compiler_params=pltpu.CompilerParams(
            dimension_semantics=("arbitrary", "arbitrary"), vmem_limit_bytes=VMEM_LIMIT),
        name="inproj",
    )(x2d, nw, w_main, w_small)


AUG0 = HEAD_DIM
VT_ROWS = HEAD_DIM + 16


def _cum_logf(logf128, carry_ref, ltri_ref):
    tp = logf128.shape[0]
    c = carry_ref[0:1, :] + _dot3(ltri_ref[...], logf128)
    carry_ref[...] = jnp.broadcast_to(c[tp - 1:tp, :], carry_ref.shape)
    return c


def _c_parts(c):
    hi, mid, lo = _split3(c)
    lane = _lane_iota(c.shape)
    h = N_HEADS_ATT
    packed = jnp.where(
        lane < h, hi.astype(F32),
        jnp.where(lane < 2 * h, pltpu.roll(mid.astype(F32), h, 1),
                  jnp.where(lane < 3 * h, pltpu.roll(lo.astype(F32), 2 * h, 1),
                            jnp.where(lane == 3 * h, 1.0, 0.0))))
    return packed.astype(BF16)


def _head_rows(x, extras, out_ref, scale):
    lane = _lane_iota((x.shape[0], LANES))
    for hp in range(N_HEADS_ATT // 2):
        pair = x[:, LANES * hp:LANES * (hp + 1)] * scale
        for o in range(2):
            h = 2 * hp + o
            xh = pair if o == 0 else pltpu.roll(pair, HEAD_DIM, 1)
            ex = extras[:, LANES * h:LANES * (h + 1)]
            out_ref[0, h] = jnp.where(lane < HEAD_DIM, xh, ex).astype(BF16)


def _write_vt(v, vt_ref):
    vt = v.T
    tail_shape = (VT_ROWS - HEAD_DIM, v.shape[0])
    tail = jnp.where(_row_iota(tail_shape) == 0, 1.0, 0.0).astype(BF16)
    for h in range(N_HEADS_ATT):
        vt_ref[0, h, 0:HEAD_DIM, :] = vt[HEAD_DIM * h:HEAD_DIM * (h + 1), :].astype(BF16)
        vt_ref[0, h, HEAD_DIM:VT_ROWS, :] = tail


def _head_rmsnorm(x, g_ref, gt_ref, w):
    sq = x * x
    hi = sq.astype(BF16)
    lo = (sq - hi.astype(F32)).astype(BF16)
    ssq = _dot(hi, g_ref[...]) + _dot(lo, g_ref[...])
    rs = lax.rsqrt(ssq * (1.0 / HEAD_DIM) + EPS)
    rs_e = _dot3_rhs(rs, gt_ref[...])
    return (x * rs_e) * w


def _prep_new_kernel(q_ref, k_ref, v_ref, small_ref, c0_ref, bf_ref, qw_ref, kw_ref,
                     g_ref, gt_ref, selq_ref, selk_ref, ltri_ref, *rest, aliased):
    if aliased:
        rest = rest[2:]
    kout_ref, logf_ref, qaug_ref, kaug_ref, vt_ref, cend_ref, carry_ref = rest
    t = pl.program_id(1)

    @pl.when(t == 0)
    def _():
        carry_ref[...] = c0_ref[0]

    logf = _log_sigmoid(small_ref[...] + bf_ref[...])
    logf_ref[...] = logf[:, SMALL_F0:SMALL_F0 + N_HEADS_ATT]
    c = _cum_logf(logf, carry_ref, ltri_ref)
    cend_ref[0] = carry_ref[...]
    cp = _c_parts(c * LOG2E)

    qn = _head_rmsnorm(q_ref[...], g_ref, gt_ref, qw_ref[...])
    _head_rows(qn, _dot(cp, selq_ref[...]), qaug_ref, ATT_SCALE * LOG2E)
    kn = _head_rmsnorm(k_ref[...], g_ref, gt_ref, kw_ref[...])
    _head_rows(kn, _dot(cp, selk_ref[...]), kaug_ref, 1.0)
    kout_ref[...] = kn
    _write_vt(v_ref[...], vt_ref)


def _prep_cache_kernel(k_ref, v_ref, logf_ref, c0_ref, selk_ref, ltri_ref,
                       kaug_ref, vt_ref, cend_ref, carry_ref):
    t = pl.program_id(1)

    @pl.when(t == 0)
    def _():
        carry_ref[...] = c0_ref[0]

    c = _cum_logf(logf_ref[0], carry_ref, ltri_ref)
    cend_ref[0] = carry_ref[...]
    _head_rows(k_ref[0], _dot(_c_parts(c * LOG2E), selk_ref[...]), kaug_ref, 1.0)
    _write_vt(v_ref[0], vt_ref)


def _aug_selectors():
    h = N_HEADS_ATT
    rows = jnp.arange(LANES)[:, None]
    cols = jnp.arange(h * LANES)[None, :]
    head = cols // LANES
    pos = cols % LANES
    one_row = rows == 3 * h
    selq = jnp.zeros((LANES, h * LANES), F32)
    selk = jnp.zeros((LANES, h * LANES), F32)
    for part in range(3):
        part_row = rows == part * h + head
        selq = selq + jnp.where(part_row & (pos == AUG0 + part), 1.0, 0.0)
        selq = selq + jnp.where(one_row & (pos == AUG0 + 3 + part), 1.0, 0.0)
        selk = selk + jnp.where(one_row & (pos == AUG0 + part), 1.0, 0.0)
        selk = selk - jnp.where(part_row & (pos == AUG0 + 3 + part), 1.0, 0.0)
    return selq.astype(BF16), selk.astype(BF16)


def _head_group_matrices():
    j = jnp.arange(ATT_WIDTH)[:, None] // HEAD_DIM
    h = jnp.arange(LANES)[None, :]
    g = jnp.where(j == h, 1.0, 0.0).astype(BF16)
    return g, g.T


def _ltri(n, block=None):
    r = jnp.arange(n)[:, None]
    c = jnp.arange(n)[None, :]
    keep = c <= r
    if block is not None:
        keep = keep & (r // block == c // block)
    return jnp.where(keep, 1.0, 0.0).astype(BF16)


def _prep_new(qk, v, small, c0, bf128, qw, kw, batch, seq, tk, bufs=None, t_off_blocks=0):
    nt = seq // tk
    h = N_HEADS_ATT
    g, gt = _head_group_matrices()
    selq, selk = _aug_selectors()
    ltri = _ltri(tk)
    rows = batch * seq
    const = lambda shape: pl.BlockSpec(shape, lambda b, t: tuple(0 for _ in shape))
    in_specs = [
        pl.BlockSpec((tk, ATT_WIDTH), lambda b, t: (b * nt + t, 0)),
        pl.BlockSpec((tk, ATT_WIDTH), lambda b, t: (b * nt + t, 1)),
        pl.BlockSpec((tk, ATT_WIDTH), lambda b, t: (b * nt + t, 0)),
        pl.BlockSpec((tk, LANES), lambda b, t: (b * nt + t, 0)),
        pl.BlockSpec((1, SUBLANES, LANES), lambda b, t: (b, 0, 0)),
        const((1, LANES)), const((1, ATT_WIDTH)), const((1, ATT_WIDTH)),
        const((ATT_WIDTH, LANES)), const((LANES, ATT_WIDTH)),
        const((LANES, h * LANES)), const((LANES, h * LANES)), const((tk, tk)),
    ]
    args = [qk, qk, v, small, c0, bf128, qw, kw, g, gt, selq, selk, ltri]
    if bufs is None:
        kv_len = seq
        aliases = {}
    else:
        kv_len = bufs[0].shape[2]
        in_specs += [pl.BlockSpec(memory_space=pl.ANY), pl.BlockSpec(memory_space=pl.ANY)]
        aliases = {len(args): 3, len(args) + 1: 4}
        args += list(bufs)
    o = t_off_blocks
    out_specs = [
        pl.BlockSpec((tk, ATT_WIDTH), lambda b, t: (b * nt + t, 0)),
        pl.BlockSpec((tk, h), lambda b, t: (b * nt + t, 0)),
        pl.BlockSpec((1, h, tk, LANES), lambda b, t: (b, 0, t, 0)),
        pl.BlockSpec((1, h, tk, LANES), lambda b, t: (b, 0, t + o, 0)),
        pl.BlockSpec((1, h, VT_ROWS, tk), lambda b, t: (b, 0, 0, t + o)),
        pl.BlockSpec((1, SUBLANES, LANES), lambda b, t: (b, 0, 0)),
    ]
    out_shape = [
        jax.ShapeDtypeStruct((rows, ATT_WIDTH), F32),
        jax.ShapeDtypeStruct((rows, h), F32),
        jax.ShapeDtypeStruct((batch, h, seq, LANES), BF16),
        jax.ShapeDtypeStruct((batch, h, kv_len, LANES), BF16),
        jax.ShapeDtypeStruct((batch, h, VT_ROWS, kv_len), BF16),
        jax.ShapeDtypeStruct((batch, SUBLANES, LANES), F32),
    ]
    return pl.pallas_call(
        functools.partial(_prep_new_kernel, aliased=bufs is not None),
        grid=(batch, nt),
        in_specs=in_specs,
        out_specs=out_specs,
        out_shape=out_shape,
        scratch_shapes=[pltpu.VMEM((SUBLANES, LANES), F32)],
        input_output_aliases=aliases,
        compiler_params=pltpu.CompilerParams(
            dimension_semantics=("arbitrary", "arbitrary"), vmem_limit_bytes=VMEM_LIMIT),
        name="prep_new",
    )(*args)


def _prep_cache(cache_k, cache_v, cache_logf128, c0, tk, kv_len):
    batch, past, _ = cache_k.shape
    nt = past // tk
    h = N_HEADS_ATT
    _, selk = _aug_selectors()
    ltri = _ltri(tk)
    const = lambda shape: pl.BlockSpec(shape, lambda b, t: tuple(0 for _ in shape))
    return pl.pallas_call(
        _prep_cache_kernel,
        grid=(batch, nt),
        in_specs=[
            pl.BlockSpec((1, tk, ATT_WIDTH), lambda b, t: (b, t, 0)),
            pl.BlockSpec((1, tk, ATT_WIDTH), lambda b, t: (b, t, 0)),
            pl.BlockSpec((1, tk, LANES), lambda b, t: (b, t, 0)),
            pl.BlockSpec((1, SUBLANES, LANES), lambda b, t: (b, 0, 0)),
            const((LANES, h * LANES)), const((tk, tk)),
        ],
        out_specs=[
            pl.BlockSpec((1, h, tk, LANES), lambda b, t: (b, 0, t, 0)),
            pl.BlockSpec((1, h, VT_ROWS, tk), lambda b, t: (b, 0, 0, t)),
            pl.BlockSpec((1, SUBLANES, LANES), lambda b, t: (b, 0, 0)),
        ],
        out_shape=[
            jax.ShapeDtypeStruct((batch, h, kv_len, LANES), BF16),
            jax.ShapeDtypeStruct((batch, h, VT_ROWS, kv_len), BF16),
            jax.ShapeDtypeStruct((batch, SUBLANES, LANES), F32),
        ],
        scratch_shapes=[pltpu.VMEM((SUBLANES, LANES), F32)],
        compiler_params=pltpu.CompilerParams(
            dimension_semantics=("arbitrary", "arbitrary"), vmem_limit_bytes=VMEM_LIMIT),
        name="prep_cache",
    )(cache_k, cache_v, cache_logf128, c0, selk, ltri)


def _flash_kernel(q_ref, k_ref, vt_ref, o_ref, s0_ref, s1_ref, cm_ref, m_ref, acc_ref,
                  *, tq, tkf, q_off):
    qi = pl.program_id(2)
    q_start = q_off + qi * tq
    n_full = q_start // tkf
    acc_ref[...] = jnp.zeros(acc_ref.shape, F32)
    m_ref[...] = jnp.full(m_ref.shape, NEG_BIG, F32)
    bufs = (s0_ref, s1_ref)

    def scores(k_start, rows, masked, slot):
        for o in range(2):
            kc = k_ref[0, o, pl.ds(k_start, rows), :]
            st = lax.dot_general(kc, q_ref[0, o], (((1,), (1,)), ((), ())),
                                 preferred_element_type=F32)
            if masked:
                st = jnp.where(_row_iota(st.shape) <= _lane_iota(st.shape), st, NEG_BIG)
            bufs[slot][o, 0:rows, :] = st
            cm_ref[slot, o] = jnp.max(st, axis=0, keepdims=True)

    def consume(k_start, rows, slot):
        for o in range(2):
            m_prev = m_ref[o]
            m_new = jnp.maximum(m_prev, cm_ref[slot, o])
            alpha = jnp.exp2(m_prev - m_new)
            p = jnp.exp2(bufs[slot][o, 0:rows, :] - m_new)
            pv = _dot(vt_ref[0, o, :, pl.ds(k_start, rows)], p.astype(BF16))
            acc_ref[o] = alpha * acc_ref[o] + pv
            m_ref[o] = m_new

    full_start = lambda f: pl.multiple_of(f * tkf, tkf)
    diag_start = pl.multiple_of(q_start, tq)
    scores(diag_start, tq, True, 0)

    @pl.when(n_full > 0)
    def _():
        scores(full_start(0), tkf, False, 1)

    consume(diag_start, tq, 0)

    pairs = jnp.maximum(n_full - 1, 0) // 2

    def body(t, carry):
        f = 2 * t
        scores(full_start(f + 1), tkf, False, 0)
        consume(full_start(f), tkf, 1)
        scores(full_start(f + 2), tkf, False, 1)
        consume(full_start(f + 1), tkf, 0)
        return carry

    lax.fori_loop(0, pairs, body, 0)
    f_rem = 2 * pairs
    left = n_full - f_rem

    @pl.when(left == 2)
    def _():
        scores(full_start(f_rem + 1), tkf, False, 0)
        consume(full_start(f_rem), tkf, 1)
        consume(full_start(f_rem + 1), tkf, 0)

    @pl.when(left == 1)
    def _():
        consume(full_start(f_rem), tkf, 1)

    outs = [acc_ref[o, 0:HEAD_DIM, :] / acc_ref[o, HEAD_DIM:HEAD_DIM + 1, :] for o in range(2)]
    o_ref[0] = jnp.concatenate(outs, axis=0).T.astype(BF16)


def _flash(q_aug, k_aug, vt, tq, tkf, q_off):
    batch, h, tq_total, _ = q_aug.shape
    kv_len = k_aug.shape[2]
    nq = tq_total // tq
    return pl.pallas_call(
        functools.partial(_flash_kernel, tq=tq, tkf=tkf, q_off=q_off),
        grid=(batch, h // 2, nq),
        in_specs=[
            pl.BlockSpec((1, 2, tq, LANES), lambda b, hp, qi: (b, hp, qi, 0)),
            pl.BlockSpec((1, 2, kv_len, LANES), lambda b, hp, qi: (b, hp, 0, 0)),
            pl.BlockSpec((1, 2, VT_ROWS, kv_len), lambda b, hp, qi: (b, hp, 0, 0)),
        ],
        out_specs=pl.BlockSpec((1, tq, LANES), lambda b, hp, qi: (b, qi, hp)),
        out_shape=jax.ShapeDtypeStruct((batch, tq_total, ATT_WIDTH), BF16),
        scratch_shapes=[
            pltpu.VMEM((2, max(tq, tkf), tq), F32),
            pltpu.VMEM((2, max(tq, tkf), tq), F32),
            pltpu.VMEM((2, 2, 1, tq), F32),
            pltpu.VMEM((2, 1, tq), F32),
            pltpu.VMEM((2, VT_ROWS, tq), F32),
        ],
        compiler_params=pltpu.CompilerParams(
            dimension_semantics=("arbitrary", "arbitrary", "arbitrary"),
            vmem_limit_bytes=VMEM_LIMIT),
        name="flash",
    )(q_aug, k_aug, vt)


B_LANE0 = D_INNER
C_LANE0 = D_INNER + N_GROUPS * D_STATE
PK_HI0 = SMALL_DT0
PK_MID0 = PK_HI0 + N_HEADS_SSM
PK_LO0 = PK_MID0 + N_HEADS_SSM


def _pack_head_parts(x):
    hi, mid, lo = _split3(x)
    lane = _lane_iota(x.shape)
    packed = jnp.where(
        lane < PK_MID0, hi.astype(F32),
        jnp.where(lane < PK_LO0, pltpu.roll(mid.astype(F32), N_HEADS_SSM, 1),
                  pltpu.roll(lo.astype(F32), 2 * N_HEADS_SSM, 1)))
    return packed.astype(BF16)


def _head_expand_matrix():
    rows = jnp.arange(LANES)[:, None]
    head = jnp.arange(D_INNER)[None, :] // SSM_HEAD_DIM
    hit = (rows == PK_HI0 + head) | (rows == PK_MID0 + head) | (rows == PK_LO0 + head)
    return jnp.where(hit, 1.0, 0.0).astype(BF16)


def _ssd_kernel(z_ref, xs_ref, bc_ref, small_ref, cpast_ref, h0_ref, cw_ref, cb_ref,
                dtb_ref, alog_ref, dskip_ref, nw_ref, lblk_ref, sel_ref,
                y_ref, cnew_ref, hlast_ref,
                xbuf_ref, ht_ref, ae_ref, dte_ref, *, tb, n_valid, tail_row):
    t = pl.program_id(1)
    nt = pl.num_programs(1)
    n_state_blocks = D_INNER // LANES

    @pl.when(t == 0)
    def _():
        xbuf_ref[0:SUBLANES, :] = cpast_ref[0]
        for i in range(n_state_blocks):
            ht_ref[:, LANES * i:LANES * (i + 1)] = h0_ref[0, LANES * i:LANES * (i + 1), :].T

    xbuf_ref[SUBLANES:SUBLANES + tb, 0:D_INNER] = xs_ref[...]
    xbuf_ref[SUBLANES:SUBLANES + tb, D_INNER:CONV_DIM] = bc_ref[...]

    lane = _lane_iota((tb, LANES))
    row = _row_iota((tb, LANES)) + t * tb
    live = (lane >= SMALL_DT0) & (lane < SMALL_DT0 + N_HEADS_SSM) & (row < n_valid)
    dt = jnp.where(live, _softplus(small_ref[...] + dtb_ref[...]), 0.0)
    a = dt * (-jnp.exp(alog_ref[...]))
    acum = _dot3(lblk_ref[...], a)
    ae_ref[...] = _dot(_pack_head_parts(acum), sel_ref[...])
    dte_ref[...] = _dot(_pack_head_parts(dt), sel_ref[...])

    gw = GROUP_WIDTH
    sub = _row_iota((SSD_CHUNK, gw))
    pos = _lane_iota((SSD_CHUNK, gw)) % SSM_HEAD_DIM
    diag = pos == sub
    causal = pos <= sub
    blk_row = _row_iota((gw, gw)) // SSM_HEAD_DIM
    blk_col = _lane_iota((gw, gw)) // SSM_HEAD_DIM
    same_head = blk_row == blk_col

    def conv(c0, lane0, width):
        blk = xbuf_ref[pl.ds(c0, SSD_CHUNK + SUBLANES), lane0:lane0 + width]
        out = cb_ref[:, lane0:lane0 + width]
        for i in range(CONV_W):
            shifted = blk if i == CONV_W - 1 else pltpu.roll(blk, CONV_W - 1 - i, 0)
            out = out + shifted[SUBLANES:, :] * cw_ref[i:i + 1, lane0:lane0 + width]
        return _silu(out)

    def chunk(c, carry):
        c0 = pl.multiple_of(c * SSD_CHUNK, SSD_CHUNK)
        rows = pl.ds(c0, SSD_CHUNK)
        for g in range(N_GROUPS):
            lanes = slice(gw * g, gw * (g + 1))
            xs = conv(c0, gw * g, gw)
            bm = conv(c0, B_LANE0 + D_STATE * g, D_STATE)
            cm = conv(c0, C_LANE0 + D_STATE * g, D_STATE)
            ae = ae_ref[rows, lanes]
            xdt = xs * dte_ref[rows, lanes]
            bm16 = bm.astype(BF16)
            cm16 = cm.astype(BF16)

            b4 = jnp.concatenate([bm16] * HEADS_PER_GROUP, axis=0)
            cb4 = lax.dot_general(cm16, b4, (((1,), (1,)), ((), ())), preferred_element_type=F32)
            a_row = jnp.sum(jnp.where(diag, ae, 0.0), axis=0, keepdims=True)
            decay = jnp.exp(jnp.where(causal, ae - a_row, NEG_BIG))
            xdt16 = xdt.astype(BF16)
            x4 = jnp.where(same_head, jnp.concatenate([xdt16] * HEADS_PER_GROUP, axis=0),
                           jnp.zeros((), BF16))
            y = _dot((cb4 * decay).astype(BF16), x4)

            hprev = ht_ref[:, lanes]
            y = y + jnp.exp(ae) * _dot(cm16, hprev.astype(BF16))
            a_last = ae[SSD_CHUNK - 1:SSD_CHUNK, :]
            xw = (xdt * jnp.exp(a_last - ae)).astype(BF16)
            upd = lax.dot_general(bm16, xw, (((0,), (0,)), ((), ())), preferred_element_type=F32)
            ht_ref[:, lanes] = hprev * jnp.exp(a_last) + upd

            y = y + xs * dskip_ref[:, lanes]
            y = y * _silu(z_ref[rows, lanes])
            ms = jnp.mean(y * y, axis=-1, keepdims=True)
            y_ref[rows, lanes] = ((y * lax.rsqrt(ms + EPS)) * nw_ref[:, lanes]).astype(BF16)
        return carry

    lax.fori_loop(0, tb // SSD_CHUNK, chunk, 0)

    cnew_ref[0] = xbuf_ref[tail_row:tail_row + SUBLANES, :]
    xbuf_ref[0:SUBLANES, :] = xbuf_ref[tb:tb + SUBLANES, :]

    @pl.when(t == nt - 1)
    def _():
        for i in range(n_state_blocks):
            hlast_ref[0, LANES * i:LANES * (i + 1), :] = ht_ref[:, LANES * i:LANES * (i + 1)].T


def _ssd(zxg, small, conv_past8, h0, cw, cb, dtb128, alog128, dskip_e, nw, batch, seq, tb, n_valid):
    nt = seq // tb
    rows = batch * seq
    tail_row = n_valid - (seq - tb)
    lblk = _ltri(tb, SSD_CHUNK)
    sel = _head_expand_matrix()
    const = lambda shape: pl.BlockSpec(shape, lambda b, t: tuple(0 for _ in shape))
    return pl.pallas_call(
        functools.partial(_ssd_kernel, tb=tb, n_valid=n_valid, tail_row=tail_row),
        grid=(batch, nt),
        in_specs=[
            pl.BlockSpec((tb, D_INNER), lambda b, t: (b * nt + t, 0)),
            pl.BlockSpec((tb, D_INNER), lambda b, t: (b * nt + t, 1)),
            pl.BlockSpec((tb, D_INNER), lambda b, t: (b * nt + t, 2)),
            pl.BlockSpec((tb, LANES), lambda b, t: (b * nt + t, 0)),
            pl.BlockSpec((1, SUBLANES, CONV_DIM), lambda b, t: (b, 0, 0)),
            pl.BlockSpec((1, D_INNER, D_STATE), lambda b, t: (b, 0, 0)),
            const((CONV_W, CONV_DIM)), const((1, CONV_DIM)),
            const((1, LANES)), const((1, LANES)), const((1, D_INNER)), const((1, D_INNER)),
            const((tb, tb)), const((LANES, D_INNER)),
        ],
        out_specs=[
            pl.BlockSpec((tb, D_INNER), lambda b, t: (b * nt + t, 0)),
            pl.BlockSpec((1, SUBLANES, CONV_DIM), lambda b, t: (b, 0, 0)),
            pl.BlockSpec((1, D_INNER, D_STATE), lambda b, t: (b, 0, 0)),
        ],
        out_shape=[
            jax.ShapeDtypeStruct((rows, D_INNER), BF16),
            jax.ShapeDtypeStruct((batch, SUBLANES, CONV_DIM), F32),
            jax.ShapeDtypeStruct((batch, D_INNER, D_STATE), F32),
        ],
        scratch_shapes=[
            pltpu.VMEM((tb + SUBLANES, CONV_DIM), F32),
            pltpu.VMEM((D_STATE, D_INNER), F32),
            pltpu.VMEM((tb, D_INNER), F32),
            pltpu.VMEM((tb, D_INNER), F32),
        ],
        compiler_params=pltpu.CompilerParams(
            dimension_semantics=("arbitrary", "arbitrary"), vmem_limit_bytes=VMEM_LIMIT),
        name="ssd",
    )(zxg, zxg, zxg, small, conv_past8, h0, cw, cb, dtb128, alog128, dskip_e, nw, lblk, sel)


FF_CHUNK = 1024


def _merge_kernel(att_ref, ys_ref, ga_ref, gs_ref, x_ref, wpa_ref, wpb_ref, wout_ref,
                  n2_ref, wup_ref, wdn_ref, o_ref):
    merged = (_sigmoid(ga_ref[...]) * _dot(att_ref[...], wpa_ref[...])
              + _sigmoid(gs_ref[...]) * _dot(ys_ref[...], wpb_ref[...]))
    x1 = x_ref[...] + _dot(merged.astype(BF16), wout_ref[...])
    ms = jnp.mean(x1 * x1, axis=-1, keepdims=True)
    hn = ((x1 * lax.rsqrt(ms + EPS)) * n2_ref[...]).astype(BF16)
    acc = x1
    for c in range(D_FF // FF_CHUNK):
        cols = slice(FF_CHUNK * c, FF_CHUNK * (c + 1))
        up = jnp.maximum(_dot(hn, wup_ref[:, cols]), 0.0)
        acc = acc + _dot((up * up).astype(BF16), wdn_ref[cols, :])
    o_ref[...] = acc


def _merge(att, ys, zxg, x2d, wpa, wpb, wout, n2, wup, wdn, tm):
    rows = x2d.shape[0]
    ga_blk = 6
    gs_blk = 7
    resident = lambda shape: pl.BlockSpec(shape, lambda i: tuple(0 for _ in shape),
                                          pipeline_mode=pl.Buffered(1))
    return pl.pallas_call(
        _merge_kernel,
        grid=(rows // tm,),
        in_specs=[
            pl.BlockSpec((tm, ATT_WIDTH), lambda i: (i, 0)),
            pl.BlockSpec((tm, D_INNER), lambda i: (i, 0)),
            pl.BlockSpec((tm, D_MODEL), lambda i: (i, ga_blk)),
            pl.BlockSpec((tm, D_MODEL), lambda i: (i, gs_blk)),
            pl.BlockSpec((tm, D_MODEL), lambda i: (i, 0)),
            resident((ATT_WIDTH, D_MODEL)), resident((D_INNER, D_MODEL)),
            resident((D_MODEL, D_MODEL)), resident((1, D_MODEL)),
            resident((D_MODEL, D_FF)), resident((D_FF, D_MODEL)),
        ],
        out_specs=pl.BlockSpec((tm, D_MODEL), lambda i: (i, 0)),
        out_shape=jax.ShapeDtypeStruct((rows, D_MODEL), F32),
        compiler_params=pltpu.CompilerParams(
            dimension_semantics=("arbitrary",), vmem_limit_bytes=VMEM_LIMIT),
        name="merge",
    )(att, ys, zxg, zxg, x2d, wpa, wpb, wout, n2, wup, wdn)


def _pick_tile(n, prefs):
    for p in prefs:
        if n % p == 0:
            return p
    raise ValueError(f"no tile in {prefs} divides {n}")


def _pad_lanes(vec, lane0):
    return jnp.zeros((1, LANES), F32).at[0, lane0:lane0 + vec.shape[0]].set(vec.astype(F32))


def _layer(x, past, p, n_valid):
    batch, seq, _ = x.shape
    rows = batch * seq
    x2d = x.reshape(rows, D_MODEL)

    tm = _pick_tile(rows, (1024, 512, 256, 128))
    qk, v, zxg, small = _inproj(x2d, p["norm1_w"], p["w_main"], p["w_small"], tm)

    tk = _pick_tile(seq, (512, 256, 128)) if past is None else LANES
    zeros_c = jnp.zeros((batch, SUBLANES, LANES), F32)
    if past is None:
        k_out, logf, q_aug, k_aug, vt, _ = _prep_new(
            qk, v, small, zeros_c, p["bf128"], p["qw"], p["kw"], batch, seq, tk)
        tkf, q_off = tk, 0
        conv_past8 = jnp.zeros((batch, SUBLANES, CONV_DIM), F32)
        h0 = jnp.zeros((batch, D_INNER, D_STATE), F32)
    else:
        cache_k, cache_v, cache_logf, conv_past, h0 = past
        n_past = cache_k.shape[1]
        kv_len = n_past + seq
        logf128 = jnp.pad(cache_logf.astype(F32), ((0, 0), (0, 0), (0, LANES - N_HEADS_ATT)))
        tkf = _pick_tile(n_past, (1024, 512, 256, 128))
        k_buf, vt_buf, c_past = _prep_cache(cache_k, cache_v, logf128, zeros_c, tkf, kv_len)
        k_out, logf, q_aug, k_aug, vt, _ = _prep_new(
            qk, v, small, c_past, p["bf128"], p["qw"], p["kw"], batch, seq, tk,
            bufs=(k_buf, vt_buf), t_off_blocks=n_past // tk)
        q_off = n_past
        conv_past8 = jnp.pad(conv_past.astype(F32), ((0, 0), (SUBLANES - (CONV_W - 1), 0), (0, 0)))
        h0 = h0.astype(F32).reshape(batch, D_INNER, D_STATE)

    att = _flash(q_aug, k_aug, vt, tk, tkf, q_off)

    tb = _pick_tile(seq, (256, 128))
    ys, conv_new8, h_last = _ssd(zxg, small, conv_past8, h0, p["conv_w"], p["conv_b"],
                                 p["dtb128"], p["alog128"], p["dskip_e"], p["ssm_norm_w"],
                                 batch, seq, tb, n_valid)

    tmm = _pick_tile(rows, (512, 256, 128))
    y = _merge(att.reshape(rows, ATT_WIDTH), ys, zxg, x2d, p["w_pa"], p["w_pb"], p["w_out"],
               p["norm2_w"], p["w_up"], p["w_down"], tmm)

    nv = n_valid
    y = y.reshape(batch, seq, D_MODEL)[:, :nv]
    k_new = k_out.reshape(batch, seq, N_HEADS_ATT, HEAD_DIM)[:, :nv]
    v_new = v.reshape(batch, seq, N_HEADS_ATT, HEAD_DIM)[:, :nv]
    logf_new = logf.reshape(batch, seq, N_HEADS_ATT)[:, :nv]
    conv_new = conv_new8[:, SUBLANES - (CONV_W - 1):, :]
    h_new = h_last.reshape(batch, N_HEADS_SSM, SSM_HEAD_DIM, D_STATE)
    return y, (k_new, v_new, logf_new, conv_new, h_new)


def _layer_params(l, norm1_w, w_in, b_f, q_norm_w, k_norm_w, conv_w, conv_b, dt_bias, A_log,
                  D_skip, ssm_norm_w, w_pa, w_pb, w_out, norm2_w, w_up, w_down):
    w = w_in[l]
    o = 0
    seg = {}
    for name, size in (("q", ATT_WIDTH), ("k", ATT_WIDTH), ("v", ATT_WIDTH), ("f", N_HEADS_ATT),
                       ("z", D_INNER), ("xbc", CONV_DIM), ("dt", N_HEADS_SSM),
                       ("ga", D_MODEL), ("gs", D_MODEL)):
        seg[name] = w[:, o:o + size]
        o += size
    w_main = jnp.concatenate([seg[n] for n in ("q", "k", "v", "z", "xbc", "ga", "gs")], axis=1)
    w_small = jnp.zeros((D_MODEL, LANES), w.dtype)
    w_small = w_small.at[:, SMALL_F0:SMALL_F0 + N_HEADS_ATT].set(seg["f"])
    w_small = w_small.at[:, SMALL_DT0:SMALL_DT0 + N_HEADS_SSM].set(seg["dt"])
    row = lambda a: a.astype(F32).reshape(1, -1)
    return {
        "norm1_w": row(norm1_w[l]),
        "w_main": w_main.astype(BF16),
        "w_small": w_small.astype(BF16),
        "bf128": _pad_lanes(b_f[l], SMALL_F0),
        "qw": row(jnp.tile(q_norm_w[l], N_HEADS_ATT)),
        "kw": row(jnp.tile(k_norm_w[l], N_HEADS_ATT)),
        "conv_w": conv_w[l].astype(F32),
        "conv_b": row(conv_b[l]),
        "dtb128": _pad_lanes(dt_bias[l], SMALL_DT0),
        "alog128": _pad_lanes(A_log[l], SMALL_DT0),
        "dskip_e": row(jnp.repeat(D_skip[l], SSM_HEAD_DIM)),
        "ssm_norm_w": row(ssm_norm_w[l]),
        "w_pa": w_pa[l].astype(BF16),
        "w_pb": w_pb[l].astype(BF16),
        "w_out": w_out[l].astype(BF16),
        "norm2_w": row(norm2_w[l]),
        "w_up": w_up[l].astype(BF16),
        "w_down": w_down[l].astype(BF16),
    }


def kernel(x_prompt, x_sample, cache_k, cache_v, cache_logf, state_conv, state_ssm, norm1_w, w_in,
           b_f, q_norm_w, k_norm_w, conv_w, conv_b, dt_bias, A_log, D_skip, ssm_norm_w, w_pa, w_pb,
           w_out, norm2_w, w_up, w_down):
    depth = w_in.shape[0]
    dec_batch, dec_seq, _ = x_sample.shape
    dec_pad = -(-dec_seq // LANES) * LANES
    yp = x_prompt
    ys = jnp.pad(x_sample, ((0, 0), (0, dec_pad - dec_seq), (0, 0)))
    states_p, states_s = [], []
    for l in range(depth):
        p = _layer_params(l, norm1_w, w_in, b_f, q_norm_w, k_norm_w, conv_w, conv_b, dt_bias,
                          A_log, D_skip, ssm_norm_w, w_pa, w_pb, w_out, norm2_w, w_up, w_down)
        yp, st_p = _layer(yp, None, p, yp.shape[1])
        n_past = cache_k.shape[2]
        past = (cache_k[l].reshape(dec_batch, n_past, ATT_WIDTH),
                cache_v[l].reshape(dec_batch, n_past, ATT_WIDTH),
                cache_logf[l], state_conv[l], state_ssm[l])
        ys_l, st_s = _layer(ys, past, p, dec_seq)
        states_p.append(st_p)
        states_s.append(st_s)
        if l + 1 < depth:
            ys = jnp.pad(ys_l, ((0, 0), (0, dec_pad - dec_seq), (0, 0)))
        else:
            ys = ys_l
    stack = lambda states, i: jnp.stack([s[i] for s in states], axis=0)
    return (yp, ys,
            stack(states_p, 0), stack(states_p, 1), stack(states_p, 2), stack(states_p, 3),
            stack(states_p, 4),
            stack(states_s, 0), stack(states_s, 1), stack(states_s, 2), stack(states_s, 3),
            stack(states_s, 4))
```

```python
import functools

import jax
import jax.numpy as jnp
from jax import lax
from jax.experimental import pallas as pl
from jax.experimental.pallas import tpu as pltpu

F32 = jnp.float32
BF16 = jnp.bfloat16

D_MODEL = 1024
N_HEADS_ATT = 16
HEAD_DIM = 64
ATT_WIDTH = N_HEADS_ATT * HEAD_DIM
ATT_SCALE = HEAD_DIM ** -0.5
D_INNER = 2048
SSM_HEAD_DIM = 64
N_HEADS_SSM = D_INNER // SSM_HEAD_DIM
N_GROUPS = 8
HEADS_PER_GROUP = N_HEADS_SSM // N_GROUPS
GROUP_WIDTH = D_INNER // N_GROUPS
D_STATE = 128
CONV_W = 4
CONV_DIM = D_INNER + 2 * N_GROUPS * D_STATE
D_FF = 4 * D_MODEL
EPS = 1e-6
SSD_CHUNK = 64

LANES = 128
SUBLANES = 8
NEG_BIG = -1e30
LOG2E = 1.4426950408889634
VMEM_LIMIT = 56 * 1024 * 1024

SMALL_F0 = 0
SMALL_DT0 = N_HEADS_ATT


def _split3(x):
    hi = x.astype(BF16)
    r1 = x - hi.astype(F32)
    mid = r1.astype(BF16)
    lo = (r1 - mid.astype(F32)).astype(BF16)
    return hi, mid, lo


def _dot(a, b):
    return jnp.dot(a, b, preferred_element_type=F32)


def _dot3(lhs_bf16, x):
    hi, mid, lo = _split3(x)
    return _dot(lhs_bf16, hi) + _dot(lhs_bf16, mid) + _dot(lhs_bf16, lo)


def _dot3_rhs(x, rhs_bf16):
    hi, mid, lo = _split3(x)
    return _dot(hi, rhs_bf16) + _dot(mid, rhs_bf16) + _dot(lo, rhs_bf16)


def _sigmoid(x):
    return 0.5 * jnp.tanh(0.5 * x) + 0.5


def _silu(x):
    h = 0.5 * x
    return h + h * jnp.tanh(h)


def _softplus(x):
    return jnp.maximum(x, 0.0) + jnp.log1p(jnp.exp(-jnp.abs(x)))


def _log_sigmoid(x):
    return jnp.minimum(x, 0.0) - jnp.log1p(jnp.exp(-jnp.abs(x)))


def _lane_iota(shape):
    return lax.broadcasted_iota(jnp.int32, shape, len(shape) - 1)


def _row_iota(shape):
    return lax.broadcasted_iota(jnp.int32, shape, len(shape) - 2)


IN_TN = 1024
IN_QK_TILES = 2
IN_V_TILE = 2
IN_ZXG_TILE0 = 3
IN_MAIN_TILES = 11


def _inproj_kernel(x_ref, nw_ref, wm_ref, ws_ref, qk_ref, v_ref, zxg_ref, small_ref, xn_ref):
    j = pl.program_id(1)

    @pl.when(j == 0)
    def _():
        x = x_ref[...]
        ms = jnp.mean(x * x, axis=-1, keepdims=True)
        xn = (x * lax.rsqrt(ms + EPS)) * nw_ref[...]
        xn_ref[...] = xn.astype(BF16)
        small_ref[...] = _dot(xn_ref[...], ws_ref[...])

    acc = _dot(xn_ref[...], wm_ref[...])

    @pl.when(j < IN_QK_TILES)
    def _():
        qk_ref[...] = acc

    @pl.when(j == IN_V_TILE)
    def _():
        v_ref[...] = acc

    @pl.when(j >= IN_ZXG_TILE0)
    def _():
        zxg_ref[...] = acc


def _inproj(x2d, nw, w_main, w_small, tm):
    rows = x2d.shape[0]
    grid = (rows // tm, IN_MAIN_TILES)
    n_zxg = IN_MAIN_TILES - IN_ZXG_TILE0
    return pl.pallas_call(
        _inproj_kernel,
        grid=grid,
        in_specs=[
            pl.BlockSpec((tm, D_MODEL), lambda i, j: (i, 0)),
            pl.BlockSpec((1, D_MODEL), lambda i, j: (0, 0)),
            pl.BlockSpec((D_MODEL, IN_TN), lambda i, j: (0, j)),
            pl.BlockSpec((D_MODEL, LANES), lambda i, j: (0, 0)),
        ],
        out_specs=[
            pl.BlockSpec((tm, IN_TN), lambda i, j: (i, jnp.minimum(j, IN_QK_TILES - 1))),
            pl.BlockSpec((tm, IN_TN), lambda i, j: (i, 0)),
            pl.BlockSpec((tm, IN_TN), lambda i, j: (i, jnp.clip(j - IN_ZXG_TILE0, 0, n_zxg - 1))),
            pl.BlockSpec((tm, LANES), lambda i, j: (i, 0)),
        ],
        out_shape=[
            jax.ShapeDtypeStruct((rows, IN_QK_TILES * IN_TN), F32),
            jax.ShapeDtypeStruct((rows, IN_TN), F32),
            jax.ShapeDtypeStruct((rows, n_zxg * IN_TN), F32),
            jax.ShapeDtypeStruct((rows, LANES), F32),
        ],
        scratch_shapes=[pltpu.VMEM((tm, D_MODEL), BF16)],
        compiler_params=pltpu.CompilerParams(
            dimension_semantics=("arbitrary", "arbitrary"), vmem_limit_bytes=VMEM_LIMIT),
        name="inproj",
    )(x2d, nw, w_main, w_small)


AUG0 = HEAD_DIM
AUG_SHIFT = AUG0 + 6
CP_ONE = 3 * N_HEADS_ATT
CP_SHIFT = CP_ONE + 1
SHIFT_MAX = 45.0
SHIFT_MARGIN = 1.03
VT_ROWS = HEAD_DIM + 16


def _cum_logf(logf128, carry_ref, ltri_ref):
    tp = logf128.shape[0]
    c = carry_ref[0:1, :] + _dot3(ltri_ref[...], logf128)
    carry_ref[...] = jnp.broadcast_to(c[tp - 1:tp, :], carry_ref.shape)
    return c


def _c_parts(c, shift=None):
    hi, mid, lo = _split3(c)
    lane = _lane_iota(c.shape)
    h = N_HEADS_ATT
    tail = jnp.where(lane == CP_ONE, 1.0, 0.0)
    if shift is not None:
        tail = jnp.where(lane == CP_SHIFT, shift, tail)
    packed = jnp.where(
        lane < h, hi.astype(F32),
        jnp.where(lane < 2 * h, pltpu.roll(mid.astype(F32), h, 1),
                  jnp.where(lane < 3 * h, pltpu.roll(lo.astype(F32), 2 * h, 1), tail)))
    return packed.astype(BF16)


def _head_rows(x, extras, out_ref, scale):
    lane = _lane_iota((x.shape[0], LANES))
    for hp in range(N_HEADS_ATT // 2):
        pair = x[:, LANES * hp:LANES * (hp + 1)] * scale
        for o in range(2):
            h = 2 * hp + o
            xh = pair if o == 0 else pltpu.roll(pair, HEAD_DIM, 1)
            ex = extras[:, LANES * h:LANES * (h + 1)]
            out_ref[0, h] = jnp.where(lane < HEAD_DIM, xh, ex).astype(BF16)


def _write_vt(v, vt_ref):
    vt = v.T
    tail_shape = (VT_ROWS - HEAD_DIM, v.shape[0])
    tail = jnp.where(_row_iota(tail_shape) == 0, 1.0, 0.0).astype(BF16)
    for h in range(N_HEADS_ATT):
        vt_ref[0, h, 0:HEAD_DIM, :] = vt[HEAD_DIM * h:HEAD_DIM * (h + 1), :].astype(BF16)
        vt_ref[0, h, HEAD_DIM:VT_ROWS, :] = tail


def _head_rmsnorm(x, g_ref, gt_ref, w):
    sq = x * x
    hi = sq.astype(BF16)
    lo = (sq - hi.astype(F32)).astype(BF16)
    ssq = _dot(hi, g_ref[...]) + _dot(lo, g_ref[...])
    rs = lax.rsqrt(ssq * (1.0 / HEAD_DIM) + EPS)
    rs_e = _dot3_rhs(rs, gt_ref[...])
    return (x * rs_e) * w


def _prep_new_kernel(q_ref, k_ref, v_ref, small_ref, c0_ref, shift_ref, bf_ref, qw_ref, kw_ref,
                     g_ref, gt_ref, selq_ref, selk_ref, ltri_ref, *rest, aliased):
    if aliased:
        rest = rest[2:]
    kout_ref, logf_ref, qaug_ref, kaug_ref, vt_ref, cend_ref, carry_ref = rest
    t = pl.program_id(1)

    @pl.when(t == 0)
    def _():
        carry_ref[...] = c0_ref[0]

    logf = _log_sigmoid(small_ref[...] + bf_ref[...])
    logf_ref[...] = logf[:, SMALL_F0:SMALL_F0 + N_HEADS_ATT]
    c = _cum_logf(logf, carry_ref, ltri_ref)
    cend_ref[0] = carry_ref[...]
    cp = _c_parts(c * LOG2E, shift_ref[0, 0:1, :])

    qn = _head_rmsnorm(q_ref[...], g_ref, gt_ref, qw_ref[...])
    _head_rows(qn, _dot(cp, selq_ref[...]), qaug_ref, ATT_SCALE * LOG2E)
    kn = _head_rmsnorm(k_ref[...], g_ref, gt_ref, kw_ref[...])
    _head_rows(kn, _dot(cp, selk_ref[...]), kaug_ref, 1.0)
    kout_ref[...] = kn
    _write_vt(v_ref[...], vt_ref)


def _prep_cache_kernel(k_ref, v_ref, logf_ref, c0_ref, g_ref, selk_ref, ltri_ref,
                       kaug_ref, vt_ref, cend_ref, kmax_ref, carry_ref):
    t = pl.program_id(1)

    @pl.when(t == 0)
    def _():
        carry_ref[...] = c0_ref[0]
        kmax_ref[...] = jnp.zeros(kmax_ref.shape, F32)

    c = _cum_logf(logf_ref[0], carry_ref, ltri_ref)
    cend_ref[0] = carry_ref[...]
    k = k_ref[0]
    _head_rows(k, _dot(_c_parts(c * LOG2E), selk_ref[...]), kaug_ref, 1.0)
    _write_vt(v_ref[0], vt_ref)
    sq = k * k
    hi = sq.astype(BF16)
    lo = (sq - hi.astype(F32)).astype(BF16)
    ssq = _dot(hi, g_ref[...]) + _dot(lo, g_ref[...])
    kmax_ref[0] = jnp.maximum(kmax_ref[0], jnp.max(ssq, axis=0, keepdims=True))


def _aug_selectors():
    h = N_HEADS_ATT
    rows = jnp.arange(LANES)[:, None]
    cols = jnp.arange(h * LANES)[None, :]
    head = cols // LANES
    pos = cols % LANES
    one_row = rows == CP_ONE
    selq = jnp.zeros((LANES, h * LANES), F32)
    selk = jnp.zeros((LANES, h * LANES), F32)
    for part in range(3):
        part_row = rows == part * h + head
        selq = selq + jnp.where(part_row & (pos == AUG0 + part), 1.0, 0.0)
        selq = selq + jnp.where(one_row & (pos == AUG0 + 3 + part), 1.0, 0.0)
        selk = selk + jnp.where(one_row & (pos == AUG0 + part), 1.0, 0.0)
        selk = selk - jnp.where(part_row & (pos == AUG0 + 3 + part), 1.0, 0.0)
    selq = selq - jnp.where((rows == CP_SHIFT) & (pos == AUG_SHIFT), 1.0, 0.0)
    selk = selk + jnp.where(one_row & (pos == AUG_SHIFT), 1.0, 0.0)
    return selq.astype(BF16), selk.astype(BF16)


def _head_group_matrices():
    j = jnp.arange(ATT_WIDTH)[:, None] // HEAD_DIM
    h = jnp.arange(LANES)[None, :]
    g = jnp.where(j == h, 1.0, 0.0).astype(BF16)
    return g, g.T


def _ltri(n, block=None):
    r = jnp.arange(n)[:, None]
    c = jnp.arange(n)[None, :]
    keep = c <= r
    if block is not None:
        keep = keep & (r // block == c // block)
    return jnp.where(keep, 1.0, 0.0).astype(BF16)


def _prep_new(qk, v, small, c0, shift, bf128, qw, kw, batch, seq, tk, bufs=None, t_off_blocks=0):
    nt = seq // tk
    h = N_HEADS_ATT
    g, gt = _head_group_matrices()
    selq, selk = _aug_selectors()
    ltri = _ltri(tk)
    rows = batch * seq
    const = lambda shape: pl.BlockSpec(shape, lambda b, t: tuple(0 for _ in shape))
    in_specs = [
        pl.BlockSpec((tk, ATT_WIDTH), lambda b, t: (b * nt + t, 0)),
        pl.BlockSpec((tk, ATT_WIDTH), lambda b, t: (b * nt + t, 1)),
        pl.BlockSpec((tk, ATT_WIDTH), lambda b, t: (b * nt + t, 0)),
        pl.BlockSpec((tk, LANES), lambda b, t: (b * nt + t, 0)),
        pl.BlockSpec((1, SUBLANES, LANES), lambda b, t: (b, 0, 0)),
        pl.BlockSpec((1, SUBLANES, LANES), lambda b, t: (b, 0, 0)),
        const((1, LANES)), const((1, ATT_WIDTH)), const((1, ATT_WIDTH)),
        const((ATT_WIDTH, LANES)), const((LANES, ATT_WIDTH)),
        const((LANES, h * LANES)), const((LANES, h * LANES)), const((tk, tk)),
    ]
    args = [qk, qk, v, small, c0, shift, bf128, qw, kw, g, gt, selq, selk, ltri]
    if bufs is None:
        kv_len = seq
        aliases = {}
    else:
        kv_len = bufs[0].shape[2]
        in_specs += [pl.BlockSpec(memory_space=pl.ANY), pl.BlockSpec(memory_space=pl.ANY)]
        aliases = {len(args): 3, len(args) + 1: 4}
        args += list(bufs)
    o = t_off_blocks
    out_specs = [
        pl.BlockSpec((tk, ATT_WIDTH), lambda b, t: (b * nt + t, 0)),
        pl.BlockSpec((tk, h), lambda b, t: (b * nt + t, 0)),
        pl.BlockSpec((1, h, tk, LANES), lambda b, t: (b, 0, t, 0)),
        pl.BlockSpec((1, h, tk, LANES), lambda b, t: (b, 0, t + o, 0)),
        pl.BlockSpec((1, h, VT_ROWS, tk), lambda b, t: (b, 0, 0, t + o)),
        pl.BlockSpec((1, SUBLANES, LANES), lambda b, t: (b, 0, 0)),
    ]
    out_shape = [
        jax.ShapeDtypeStruct((rows, ATT_WIDTH), F32),
        jax.ShapeDtypeStruct((rows, h), F32),
        jax.ShapeDtypeStruct((batch, h, seq, LANES), BF16),
        jax.ShapeDtypeStruct((batch, h, kv_len, LANES), BF16),
        jax.ShapeDtypeStruct((batch, h, VT_ROWS, kv_len), BF16),
        jax.ShapeDtypeStruct((batch, SUBLANES, LANES), F32),
    ]
    return pl.pallas_call(
        functools.partial(_prep_new_kernel, aliased=bufs is not None),
        grid=(batch, nt),
        in_specs=in_specs,
        out_specs=out_specs,
        out_shape=out_shape,
        scratch_shapes=[pltpu.VMEM((SUBLANES, LANES), F32)],
        input_output_aliases=aliases,
        compiler_params=pltpu.CompilerParams(
            dimension_semantics=("arbitrary", "arbitrary"), vmem_limit_bytes=VMEM_LIMIT),
        name="prep_new",
    )(*args)


def _prep_cache(cache_k, cache_v, cache_logf128, c0, tk, kv_len):
    batch, past, _ = cache_k.shape
    nt = past // tk
    h = N_HEADS_ATT
    g, _ = _head_group_matrices()
    _, selk = _aug_selectors()
    ltri = _ltri(tk)
    const = lambda shape: pl.BlockSpec(shape, lambda b, t: tuple(0 for _ in shape))
    return pl.pallas_call(
        _prep_cache_kernel,
        grid=(batch, nt),
        in_specs=[
            pl.BlockSpec((1, tk, ATT_WIDTH), lambda b, t: (b, t, 0)),
            pl.BlockSpec((1, tk, ATT_WIDTH), lambda b, t: (b, t, 0)),
            pl.BlockSpec((1, tk, LANES), lambda b, t: (b, t, 0)),
            pl.BlockSpec((1, SUBLANES, LANES), lambda b, t: (b, 0, 0)),
            const((ATT_WIDTH, LANES)), const((LANES, h * LANES)), const((tk, tk)),
        ],
        out_specs=[
            pl.BlockSpec((1, h, tk, LANES), lambda b, t: (b, 0, t, 0)),
            pl.BlockSpec((1, h, VT_ROWS, tk), lambda b, t: (b, 0, 0, t)),
            pl.BlockSpec((1, SUBLANES, LANES), lambda b, t: (b, 0, 0)),
            pl.BlockSpec((1, 1, LANES), lambda b, t: (b, 0, 0)),
        ],
        out_shape=[
            jax.ShapeDtypeStruct((batch, h, kv_len, LANES), BF16),
            jax.ShapeDtypeStruct((batch, h, VT_ROWS, kv_len), BF16),
            jax.ShapeDtypeStruct((batch, SUBLANES, LANES), F32),
            jax.ShapeDtypeStruct((batch, 1, LANES), F32),
        ],
        scratch_shapes=[pltpu.VMEM((SUBLANES, LANES), F32)],
        compiler_params=pltpu.CompilerParams(
            dimension_semantics=("arbitrary", "arbitrary"), vmem_limit_bytes=VMEM_LIMIT),
        name="prep_cache",
    )(cache_k, cache_v, cache_logf128, c0, g, selk, ltri)


def _flash_kernel(q_ref, k_ref, vt_ref, o_ref, s0_ref, s1_ref, cm_ref, m_ref, acc_ref,
                  *, tq, tkf, q_off):
    qi = pl.program_id(2)
    q_start = q_off + qi * tq
    n_full = q_start // tkf
    acc_ref[...] = jnp.zeros(acc_ref.shape, F32)
    m_ref[...] = jnp.full(m_ref.shape, NEG_BIG, F32)
    bufs = (s0_ref, s1_ref)

    def scores(k_start, rows, masked, slot):
        for o in range(2):
            kc = k_ref[0, o, pl.ds(k_start, rows), :]
            st = lax.dot_general(kc, q_ref[0, o], (((1,), (1,)), ((), ())),
                                 preferred_element_type=F32)
            if masked:
                st = jnp.where(_row_iota(st.shape) <= _lane_iota(st.shape), st, NEG_BIG)
            bufs[slot][o, 0:rows, :] = st
            cm_ref[slot, o] = jnp.max(st, axis=0, keepdims=True)

    def consume(k_start, rows, slot):
        for o in range(2):
            m_prev = m_ref[o]
            m_new = jnp.maximum(m_prev, cm_ref[slot, o])
            alpha = jnp.exp2(m_prev - m_new)
            p = jnp.exp2(bufs[slot][o, 0:rows, :] - m_new)
            pv = _dot(vt_ref[0, o, :, pl.ds(k_start, rows)], p.astype(BF16))
            acc_ref[o] = alpha * acc_ref[o] + pv
            m_ref[o] = m_new

    full_start = lambda f: pl.multiple_of(f * tkf, tkf)
    diag_start = pl.multiple_of(q_start, tq)
    scores(diag_start, tq, True, 0)

    @pl.when(n_full > 0)
    def _():
        scores(full_start(0), tkf, False, 1)

    consume(diag_start, tq, 0)

    pairs = jnp.maximum(n_full - 1, 0) // 2

    def body(t, carry):
        f = 2 * t
        scores(full_start(f + 1), tkf, False, 0)
        consume(full_start(f), tkf, 1)
        scores(full_start(f + 2), tkf, False, 1)
        consume(full_start(f + 1), tkf, 0)
        return carry

    lax.fori_loop(0, pairs, body, 0)
    f_rem = 2 * pairs
    left = n_full - f_rem

    @pl.when(left == 2)
    def _():
        scores(full_start(f_rem + 1), tkf, False, 0)
        consume(full_start(f_rem), tkf, 1)
        consume(full_start(f_rem + 1), tkf, 0)

    @pl.when(left == 1)
    def _():
        consume(full_start(f_rem), tkf, 1)

    outs = [acc_ref[o, 0:HEAD_DIM, :] / acc_ref[o, HEAD_DIM:HEAD_DIM + 1, :] for o in range(2)]
    o_ref[0] = jnp.concatenate(outs, axis=0).T.astype(BF16)


FLASH_UNROLL = 4


def _flash_fixed_kernel(q_ref, k_ref, vt_ref, o_ref, pa_ref, pb_ref, acc_ref, *, tq, tkf, q_off):
    qi = pl.program_id(2)
    q_start = q_off + qi * tq
    n_full = q_start // tkf
    slot_of = lambda f_parity: pa_ref if f_parity else pb_ref

    def probs(k_start, rows, masked, p_ref):
        for o in range(2):
            kc = k_ref[0, o, pl.ds(k_start, rows), :]
            st = lax.dot_general(kc, q_ref[0, o], (((1,), (1,)), ((), ())),
                                 preferred_element_type=F32)
            if masked:
                st = jnp.where(_row_iota(st.shape) <= _lane_iota(st.shape), st, NEG_BIG)
            p_ref[o, 0:rows, :] = jnp.exp2(st).astype(BF16)

    def accumulate(k_start, rows, p_ref, first=False):
        for o in range(2):
            pv = _dot(vt_ref[0, o, :, pl.ds(k_start, rows)], p_ref[o, 0:rows, :])
            acc_ref[o] = pv if first else acc_ref[o] + pv

    full_start = lambda f: pl.multiple_of(f * tkf, tkf)
    diag_start = pl.multiple_of(q_start, tq)
    probs(diag_start, tq, True, pa_ref)

    @pl.when(n_full > 0)
    def _():
        probs(full_start(0), tkf, False, slot_of(0))

    accumulate(diag_start, tq, pa_ref, first=True)

    def step(f, parity):
        probs(full_start(f + 1), tkf, False, slot_of(1 - parity))
        accumulate(full_start(f), tkf, slot_of(parity))

    n_steps = jnp.maximum(n_full - 1, 0)

    def body(t, carry):
        for i in range(FLASH_UNROLL):
            step(FLASH_UNROLL * t + i, i % 2)
        return carry

    lax.fori_loop(0, n_steps // FLASH_UNROLL, body, 0)
    f0 = (n_steps // FLASH_UNROLL) * FLASH_UNROLL
    left = n_steps - f0
    for r in range(FLASH_UNROLL):
        @pl.when((n_full > 0) & (left == r))
        def _(r=r):
            for i in range(r):
                step(f0 + i, i % 2)
            accumulate(full_start(f0 + r), tkf, slot_of(r % 2))

    outs = [acc_ref[o, 0:HEAD_DIM, :] / acc_ref[o, HEAD_DIM:HEAD_DIM + 1, :] for o in range(2)]
    o_ref[0] = jnp.concatenate(outs, axis=0).T.astype(BF16)


def _flash(q_aug, k_aug, vt, tq, tkf, q_off, fixed_reference):
    batch, h, tq_total, _ = q_aug.shape
    kv_len = k_aug.shape[2]
    nq = tq_total // tq
    if fixed_reference:
        body = _flash_fixed_kernel
        scratch = [
            pltpu.VMEM((2, max(tq, tkf), tq), BF16),
            pltpu.VMEM((2, max(tq, tkf), tq), BF16),
            pltpu.VMEM((2, VT_ROWS, tq), F32),
        ]
    else:
        body = _flash_kernel
        scratch = [
            pltpu.VMEM((2, max(tq, tkf), tq), F32),
            pltpu.VMEM((2, max(tq, tkf), tq), F32),
            pltpu.VMEM((2, 2, 1, tq), F32),
            pltpu.VMEM((2, 1, tq), F32),
            pltpu.VMEM((2, VT_ROWS, tq), F32),
        ]
    return pl.pallas_call(
        functools.partial(body, tq=tq, tkf=tkf, q_off=q_off),
        grid=(batch, h // 2, nq),
        in_specs=[
            pl.BlockSpec((1, 2, tq, LANES), lambda b, hp, qi: (b, hp, qi, 0)),
            pl.BlockSpec((1, 2, kv_len, LANES), lambda b, hp, qi: (b, hp, 0, 0)),
            pl.BlockSpec((1, 2, VT_ROWS, kv_len), lambda b, hp, qi: (b, hp, 0, 0)),
        ],
        out_specs=pl.BlockSpec((1, tq, LANES), lambda b, hp, qi: (b, qi, hp)),
        out_shape=jax.ShapeDtypeStruct((batch, tq_total, ATT_WIDTH), BF16),
        scratch_shapes=scratch,
        compiler_params=pltpu.CompilerParams(
            dimension_semantics=("arbitrary", "arbitrary", "arbitrary"),
            vmem_limit_bytes=VMEM_LIMIT),
        name="flash_fixed" if fixed_reference else "flash",
    )(q_aug, k_aug, vt)


B_LANE0 = D_INNER
C_LANE0 = D_INNER + N_GROUPS * D_STATE
PK_HI0 = SMALL_DT0
PK_MID0 = PK_HI0 + N_HEADS_SSM
PK_LO0 = PK_MID0 + N_HEADS_SSM


def _pack_head_parts(x):
    hi, mid, lo = _split3(x)
    lane = _lane_iota(x.shape)
    packed = jnp.where(
        lane < PK_MID0, hi.astype(F32),
        jnp.where(lane < PK_LO0, pltpu.roll(mid.astype(F32), N_HEADS_SSM, 1),
                  pltpu.roll(lo.astype(F32), 2 * N_HEADS_SSM, 1)))
    return packed.astype(BF16)


def _head_expand_matrix():
    rows = jnp.arange(LANES)[:, None]
    head = jnp.arange(D_INNER)[None, :] // SSM_HEAD_DIM
    hit = (rows == PK_HI0 + head) | (rows == PK_MID0 + head) | (rows == PK_LO0 + head)
    return jnp.where(hit, 1.0, 0.0).astype(BF16)


def _ssd_kernel(z_ref, xs_ref, bc_ref, small_ref, cpast_ref, h0_ref, cw_ref, cb_ref,
                dtb_ref, alog_ref, dskip_ref, nw_ref, lblk_ref, sel_ref,
                y_ref, cnew_ref, hlast_ref,
                xbuf_ref, ht_ref, ae_ref, dte_ref, *, tb, n_valid, tail_row):
    t = pl.program_id(1)
    nt = pl.num_programs(1)
    n_state_blocks = D_INNER // LANES

    @pl.when(t == 0)
    def _():
        xbuf_ref[0:SUBLANES, :] = cpast_ref[0]
        for i in range(n_state_blocks):
            ht_ref[:, LANES * i:LANES * (i + 1)] = h0_ref[0, LANES * i:LANES * (i + 1), :].T

    xbuf_ref[SUBLANES:SUBLANES + tb, 0:D_INNER] = xs_ref[...]
    xbuf_ref[SUBLANES:SUBLANES + tb, D_INNER:CONV_DIM] = bc_ref[...]

    lane = _lane_iota((tb, LANES))
    row = _row_iota((tb, LANES)) + t * tb
    live = (lane >= SMALL_DT0) & (lane < SMALL_DT0 + N_HEADS_SSM) & (row < n_valid)
    dt = jnp.where(live, _softplus(small_ref[...] + dtb_ref[...]), 0.0)
    a = dt * (-jnp.exp(alog_ref[...]))
    acum = _dot3(lblk_ref[...], a)
    ae_ref[...] = _dot(_pack_head_parts(acum), sel_ref[...])
    dte_ref[...] = _dot(_pack_head_parts(dt), sel_ref[...])

    gw = GROUP_WIDTH
    sub = _row_iota((SSD_CHUNK, gw))
    pos = _lane_iota((SSD_CHUNK, gw)) % SSM_HEAD_DIM
    diag = pos == sub
    causal = pos <= sub
    blk_row = _row_iota((gw, gw)) // SSM_HEAD_DIM
    blk_col = _lane_iota((gw, gw)) // SSM_HEAD_DIM
    same_head = blk_row == blk_col

    def conv(c0, lane0, width):
        blk = xbuf_ref[pl.ds(c0, SSD_CHUNK + SUBLANES), lane0:lane0 + width]
        out = cb_ref[:, lane0:lane0 + width]
        for i in range(CONV_W):
            shifted = blk if i == CONV_W - 1 else pltpu.roll(blk, CONV_W - 1 - i, 0)
            out = out + shifted[SUBLANES:, :] * cw_ref[i:i + 1, lane0:lane0 + width]
        return _silu(out)

    def chunk(c, carry):
        c0 = pl.multiple_of(c * SSD_CHUNK, SSD_CHUNK)
        rows = pl.ds(c0, SSD_CHUNK)
        for g in range(N_GROUPS):
            lanes = slice(gw * g, gw * (g + 1))
            xs = conv(c0, gw * g, gw)
            bm = conv(c0, B_LANE0 + D_STATE * g, D_STATE)
            cm = conv(c0, C_LANE0 + D_STATE * g, D_STATE)
            ae = ae_ref[rows, lanes]
            xdt = xs * dte_ref[rows, lanes]
            bm16 = bm.astype(BF16)
            cm16 = cm.astype(BF16)

            b4 = jnp.concatenate([bm16] * HEADS_PER_GROUP, axis=0)
            cb4 = lax.dot_general(cm16, b4, (((1,), (1,)), ((), ())), preferred_element_type=F32)
            a_row = jnp.sum(jnp.where(diag, ae, 0.0), axis=0, keepdims=True)
            decay = jnp.exp(jnp.where(causal, ae - a_row, NEG_BIG))
            xdt16 = xdt.astype(BF16)
            x4 = jnp.where(same_head, jnp.concatenate([xdt16] * HEADS_PER_GROUP, axis=0),
                           jnp.zeros((), BF16))
            y = _dot((cb4 * decay).astype(BF16), x4)

            hprev = ht_ref[:, lanes]
            y = y + jnp.exp(ae) * _dot(cm16, hprev.astype(BF16))
            a_last = ae[SSD_CHUNK - 1:SSD_CHUNK, :]
            xw = (xdt * jnp.exp(a_last - ae)).astype(BF16)
            upd = lax.dot_general(bm16, xw, (((0,), (0,)), ((), ())), preferred_element_type=F32)
            ht_ref[:, lanes] = hprev * jnp.exp(a_last) + upd

            y = y + xs * dskip_ref[:, lanes]
            y = y * _silu(z_ref[rows, lanes])
            ms = jnp.mean(y * y, axis=-1, keepdims=True)
            y_ref[rows, lanes] = ((y * lax.rsqrt(ms + EPS)) * nw_ref[:, lanes]).astype(BF16)
        return carry

    lax.fori_loop(0, tb // SSD_CHUNK, chunk, 0)

    cnew_ref[0] = xbuf_ref[tail_row:tail_row + SUBLANES, :]
    xbuf_ref[0:SUBLANES, :] = xbuf_ref[tb:tb + SUBLANES, :]

    @pl.when(t == nt - 1)
    def _():
        for i in range(n_state_blocks):
            hlast_ref[0, LANES * i:LANES * (i + 1), :] = ht_ref[:, LANES * i:LANES * (i + 1)].T


def _ssd(zxg, small, conv_past8, h0, cw, cb, dtb128, alog128, dskip_e, nw, batch, seq, tb, n_valid):
    nt = seq // tb
    rows = batch * seq
    tail_row = n_valid - (seq - tb)
    lblk = _ltri(tb, SSD_CHUNK)
    sel = _head_expand_matrix()
    const = lambda shape: pl.BlockSpec(shape, lambda b, t: tuple(0 for _ in shape))
    return pl.pallas_call(
        functools.partial(_ssd_kernel, tb=tb, n_valid=n_valid, tail_row=tail_row),
        grid=(batch, nt),
        in_specs=[
            pl.BlockSpec((tb, D_INNER), lambda b, t: (b * nt + t, 0)),
            pl.BlockSpec((tb, D_INNER), lambda b, t: (b * nt + t, 1)),
            pl.BlockSpec((tb, D_INNER), lambda b, t: (b * nt + t, 2)),
            pl.BlockSpec((tb, LANES), lambda b, t: (b * nt + t, 0)),
            pl.BlockSpec((1, SUBLANES, CONV_DIM), lambda b, t: (b, 0, 0)),
            pl.BlockSpec((1, D_INNER, D_STATE), lambda b, t: (b, 0, 0)),
            const((CONV_W, CONV_DIM)), const((1, CONV_DIM)),
            const((1, LANES)), const((1, LANES)), const((1, D_INNER)), const((1, D_INNER)),
            const((tb, tb)), const((LANES, D_INNER)),
        ],
        out_specs=[
            pl.BlockSpec((tb, D_INNER), lambda b, t: (b * nt + t, 0)),
            pl.BlockSpec((1, SUBLANES, CONV_DIM), lambda b, t: (b, 0, 0)),
            pl.BlockSpec((1, D_INNER, D_STATE), lambda b, t: (b, 0, 0)),
        ],
        out_shape=[
            jax.ShapeDtypeStruct((rows, D_INNER), BF16),
            jax.ShapeDtypeStruct((batch, SUBLANES, CONV_DIM), F32),
            jax.ShapeDtypeStruct((batch, D_INNER, D_STATE), F32),
        ],
        scratch_shapes=[
            pltpu.VMEM((tb + SUBLANES, CONV_DIM), F32),
            pltpu.VMEM((D_STATE, D_INNER), F32),
            pltpu.VMEM((tb, D_INNER), F32),
            pltpu.VMEM((tb, D_INNER), F32),
        ],
        compiler_params=pltpu.CompilerParams(
            dimension_semantics=("arbitrary", "arbitrary"), vmem_limit_bytes=VMEM_LIMIT),
        name="ssd",
    )(zxg, zxg, zxg, small, conv_past8, h0, cw, cb, dtb128, alog128, dskip_e, nw, lblk, sel)


FF_CHUNK = 1024


def _merge_kernel(att_ref, ys_ref, ga_ref, gs_ref, x_ref, wpa_ref, wpb_ref, wout_ref,
                  n2_ref, wup_ref, wdn_ref, o_ref):
    merged = (_sigmoid(ga_ref[...]) * _dot(att_ref[...], wpa_ref[...])
              + _sigmoid(gs_ref[...]) * _dot(ys_ref[...], wpb_ref[...]))
    x1 = x_ref[...] + _dot(merged.astype(BF16), wout_ref[...])
    ms = jnp.mean(x1 * x1, axis=-1, keepdims=True)
    hn = ((x1 * lax.rsqrt(ms + EPS)) * n2_ref[...]).astype(BF16)
    acc = x1
    for c in range(D_FF // FF_CHUNK):
        cols = slice(FF_CHUNK * c, FF_CHUNK * (c + 1))
        up = jnp.maximum(_dot(hn, wup_ref[:, cols]), 0.0)
        acc = acc + _dot((up * up).astype(BF16), wdn_ref[cols, :])
    o_ref[...] = acc


def _merge(att, ys, zxg, x2d, wpa, wpb, wout, n2, wup, wdn, tm):
    rows = x2d.shape[0]
    ga_blk = 6
    gs_blk = 7
    resident = lambda shape: pl.BlockSpec(shape, lambda i: tuple(0 for _ in shape),
                                          pipeline_mode=pl.Buffered(1))
    return pl.pallas_call(
        _merge_kernel,
        grid=(rows // tm,),
        in_specs=[
            pl.BlockSpec((tm, ATT_WIDTH), lambda i: (i, 0)),
            pl.BlockSpec((tm, D_INNER), lambda i: (i, 0)),
            pl.BlockSpec((tm, D_MODEL), lambda i: (i, ga_blk)),
            pl.BlockSpec((tm, D_MODEL), lambda i: (i, gs_blk)),
            pl.BlockSpec((tm, D_MODEL), lambda i: (i, 0)),
            resident((ATT_WIDTH, D_MODEL)), resident((D_INNER, D_MODEL)),
            resident((D_MODEL, D_MODEL)), resident((1, D_MODEL)),
            resident((D_MODEL, D_FF)), resident((D_FF, D_MODEL)),
        ],
        out_specs=pl.BlockSpec((tm, D_MODEL), lambda i: (i, 0)),
        out_shape=jax.ShapeDtypeStruct((rows, D_MODEL), F32),
        compiler_params=pltpu.CompilerParams(
            dimension_semantics=("arbitrary",), vmem_limit_bytes=VMEM_LIMIT),
        name="merge",
    )(att, ys, zxg, zxg, x2d, wpa, wpb, wout, n2, wup, wdn)


def _pick_tile(n, prefs):
    for p in prefs:
        if n % p == 0:
            return p
    raise ValueError(f"no tile in {prefs} divides {n}")


def _pad_lanes(vec, lane0):
    return jnp.zeros((1, LANES), F32).at[0, lane0:lane0 + vec.shape[0]].set(vec.astype(F32))


def _layer(x, past, p, n_valid):
    batch, seq, _ = x.shape
    rows = batch * seq
    x2d = x.reshape(rows, D_MODEL)

    tm = _pick_tile(rows, (1024, 512, 256, 128))
    qk, v, zxg, small = _inproj(x2d, p["norm1_w"], p["w_main"], p["w_small"], tm)

    tk = _pick_tile(seq, (512, 256, 128)) if past is None else LANES
    zeros_c = jnp.zeros((batch, SUBLANES, LANES), F32)
    k2_new = HEAD_DIM * jnp.max(jnp.square(p["kw"]))
    if past is None:
        k2 = jnp.full((batch,), k2_new, F32)
    else:
        cache_k, cache_v, cache_logf, conv_past, h0 = past
        n_past = cache_k.shape[1]
        kv_len = n_past + seq
        logf128 = jnp.pad(cache_logf.astype(F32), ((0, 0), (0, 0), (0, LANES - N_HEADS_ATT)))
        tkf = _pick_tile(n_past, (1024, 512, 256, 128))
        k_buf, vt_buf, c_past, k2_cache = _prep_cache(cache_k, cache_v, logf128, zeros_c, tkf, kv_len)
        k2 = jnp.maximum(k2_new, jnp.max(k2_cache, axis=(1, 2)))
    q_norm = jnp.sqrt(HEAD_DIM * jnp.max(jnp.square(p["qw"])))
    shift = (SHIFT_MARGIN * ATT_SCALE * LOG2E) * q_norm * jnp.sqrt(k2)
    shift = shift.astype(BF16).astype(F32)
    shift_rows = jnp.broadcast_to(shift[:, None, None], (batch, SUBLANES, LANES))
    if past is None:
        k_out, logf, q_aug, k_aug, vt, _ = _prep_new(
            qk, v, small, zeros_c, shift_rows, p["bf128"], p["qw"], p["kw"], batch, seq, tk)
        tkf, q_off = tk, 0
        conv_past8 = jnp.zeros((batch, SUBLANES, CONV_DIM), F32)
        h0 = jnp.zeros((batch, D_INNER, D_STATE), F32)
    else:
        k_out, logf, q_aug, k_aug, vt, _ = _prep_new(
            qk, v, small, c_past, shift_rows, p["bf128"], p["qw"], p["kw"], batch, seq, tk,
            bufs=(k_buf, vt_buf), t_off_blocks=n_past // tk)
        q_off = n_past
        conv_past8 = jnp.pad(conv_past.astype(F32), ((0, 0), (SUBLANES - (CONV_W - 1), 0), (0, 0)))
        h0 = h0.astype(F32).reshape(batch, D_INNER, D_STATE)

    att = lax.cond(jnp.max(shift) <= SHIFT_MAX,
                   lambda: _flash(q_aug, k_aug, vt, tk, tkf, q_off, True),
                   lambda: _flash(q_aug, k_aug, vt, tk, tkf, q_off, False))

    tb = _pick_tile(seq, (256, 128))
    ys, conv_new8, h_last = _ssd(zxg, small, conv_past8, h0, p["conv_w"], p["conv_b"],
                                 p["dtb128"], p["alog128"], p["dskip_e"], p["ssm_norm_w"],
                                 batch, seq, tb, n_valid)

    tmm = _pick_tile(rows, (512, 256, 128))
    y = _merge(att.reshape(rows, ATT_WIDTH), ys, zxg, x2d, p["w_pa"], p["w_pb"], p["w_out"],
               p["norm2_w"], p["w_up"], p["w_down"], tmm)

    nv = n_valid
    y = y.reshape(batch, seq, D_MODEL)[:, :nv]
    k_new = k_out.reshape(batch, seq, N_HEADS_ATT, HEAD_DIM)[:, :nv]
    v_new = v.reshape(batch, seq, N_HEADS_ATT, HEAD_DIM)[:, :nv]
    logf_new = logf.reshape(batch, seq, N_HEADS_ATT)[:, :nv]
    conv_new = conv_new8[:, SUBLANES - (CONV_W - 1):, :]
    h_new = h_last.reshape(batch, N_HEADS_SSM, SSM_HEAD_DIM, D_STATE)
    return y, (k_new, v_new, logf_new, conv_new, h_new)


def _layer_params(l, norm1_w, w_in, b_f, q_norm_w, k_norm_w, conv_w, conv_b, dt_bias, A_log,
                  D_skip, ssm_norm_w, w_pa, w_pb, w_out, norm2_w, w_up, w_down):
    w = w_in[l]
    o = 0
    seg = {}
    for name, size in (("q", ATT_WIDTH), ("k", ATT_WIDTH), ("v", ATT_WIDTH), ("f", N_HEADS_ATT),
                       ("z", D_INNER), ("xbc", CONV_DIM), ("dt", N_HEADS_SSM),
                       ("ga", D_MODEL), ("gs", D_MODEL)):
        seg[name] = w[:, o:o + size]
        o += size
    w_main = jnp.concatenate([seg[n] for n in ("q", "k", "v", "z", "xbc", "ga", "gs")], axis=1)
    w_small = jnp.zeros((D_MODEL, LANES), w.dtype)
    w_small = w_small.at[:, SMALL_F0:SMALL_F0 + N_HEADS_ATT].set(seg["f"])
    w_small = w_small.at[:, SMALL_DT0:SMALL_DT0 + N_HEADS_SSM].set(seg["dt"])
    row = lambda a: a.astype(F32).reshape(1, -1)
    return {
        "norm1_w": row(norm1_w[l]),
        "w_main": w_main.astype(BF16),
        "w_small": w_small.astype(BF16),
        "bf128": _pad_lanes(b_f[l], SMALL_F0),
        "qw": row(jnp.tile(q_norm_w[l], N_HEADS_ATT)),
        "kw": row(jnp.tile(k_norm_w[l], N_HEADS_ATT)),
        "conv_w": conv_w[l].astype(F32),
        "conv_b": row(conv_b[l]),
        "dtb128": _pad_lanes(dt_bias[l], SMALL_DT0),
        "alog128": _pad_lanes(A_log[l], SMALL_DT0),
        "dskip_e": row(jnp.repeat(D_skip[l], SSM_HEAD_DIM)),
        "ssm_norm_w": row(ssm_norm_w[l]),
        "w_pa": w_pa[l].astype(BF16),
        "w_pb": w_pb[l].astype(BF16),
        "w_out": w_out[l].astype(BF16),
        "norm2_w": row(norm2_w[l]),
        "w_up": w_up[l].astype(BF16),
        "w_down": w_down[l].astype(BF16),
    }


def kernel(x_prompt, x_sample, cache_k, cache_v, cache_logf, state_conv, state_ssm, norm1_w, w_in,
           b_f, q_norm_w, k_norm_w, conv_w, conv_b, dt_bias, A_log, D_skip, ssm_norm_w, w_pa, w_pb,
           w_out, norm2_w, w_up, w_down):
    depth = w_in.shape[0]
    dec_batch, dec_seq, _ = x_sample.shape
    dec_pad = -(-dec_seq // LANES) * LANES
    yp = x_prompt
    ys = jnp.pad(x_sample, ((0, 0), (0, dec_pad - dec_seq), (0, 0)))
    states_p, states_s = [], []
    for l in range(depth):
        p = _layer_params(l, norm1_w, w_in, b_f, q_norm_w, k_norm_w, conv_w, conv_b, dt_bias,
                          A_log, D_skip, ssm_norm_w, w_pa, w_pb, w_out, norm2_w, w_up, w_down)
        yp, st_p = _layer(yp, None, p, yp.shape[1])
        n_past = cache_k.shape[2]
        past = (cache_k[l].reshape(dec_batch, n_past, ATT_WIDTH),
                cache_v[l].reshape(dec_batch, n_past, ATT_WIDTH),
                cache_logf[l], state_conv[l], state_ssm[l])
        ys_l, st_s = _layer(ys, past, p, dec_seq)
        states_p.append(st_p)
        states_s.append(st_s)
        if l + 1 < depth:
            ys = jnp.pad(ys_l, ((0, 0), (0, dec_pad - dec_seq), (0, 0)))
        else:
            ys = ys_l
    stack = lambda states, i: jnp.stack([s[i] for s in states], axis=0)
    return (yp, ys,
            stack(states_p, 0), stack(states_p, 1), stack(states_p, 2), stack(states_p, 3),
            stack(states_p, 4),
            stack(states_s, 0), stack(states_s, 1), stack(states_s, 2), stack(states_s, 3),
            stack(states_s, 4))
```

```python
import functools

import jax
import jax.numpy as jnp
from jax import lax
from jax.experimental import pallas as pl
from jax.experimental.pallas import tpu as pltpu

F32 = jnp.float32
BF16 = jnp.bfloat16

D_MODEL = 1024
N_HEADS_ATT = 16
HEAD_DIM = 64
ATT_WIDTH = N_HEADS_ATT * HEAD_DIM
ATT_SCALE = HEAD_DIM ** -0.5
D_INNER = 2048
SSM_HEAD_DIM = 64
N_HEADS_SSM = D_INNER // SSM_HEAD_DIM
N_GROUPS = 8
HEADS_PER_GROUP = N_HEADS_SSM // N_GROUPS
GROUP_WIDTH = D_INNER // N_GROUPS
D_STATE = 128
CONV_W = 4
CONV_DIM = D_INNER + 2 * N_GROUPS * D_STATE
D_FF = 4 * D_MODEL
EPS = 1e-6
SSD_CHUNK = 64

LANES = 128
SUBLANES = 8
NEG_BIG = -1e30
LOG2E = 1.4426950408889634
VMEM_LIMIT = 56 * 1024 * 1024

SMALL_F0 = 0
SMALL_DT0 = N_HEADS_ATT


def _split3(x):
    hi = x.astype(BF16)
    r1 = x - hi.astype(F32)
    mid = r1.astype(BF16)
    lo = (r1 - mid.astype(F32)).astype(BF16)
    return hi, mid, lo


def _dot(a, b):
    return jnp.dot(a, b, preferred_element_type=F32)


def _dot3(lhs_bf16, x):
    hi, mid, lo = _split3(x)
    return _dot(lhs_bf16, hi) + _dot(lhs_bf16, mid) + _dot(lhs_bf16, lo)


def _dot3_rhs(x, rhs_bf16):
    hi, mid, lo = _split3(x)
    return _dot(hi, rhs_bf16) + _dot(mid, rhs_bf16) + _dot(lo, rhs_bf16)


def _sigmoid(x):
    return 0.5 * jnp.tanh(0.5 * x) + 0.5


def _silu(x):
    h = 0.5 * x
    return h + h * jnp.tanh(h)


def _softplus(x):
    return jnp.maximum(x, 0.0) + jnp.log1p(jnp.exp(-jnp.abs(x)))


def _log_sigmoid(x):
    return jnp.minimum(x, 0.0) - jnp.log1p(jnp.exp(-jnp.abs(x)))


def _lane_iota(shape):
    return lax.broadcasted_iota(jnp.int32, shape, len(shape) - 1)


def _row_iota(shape):
    return lax.broadcasted_iota(jnp.int32, shape, len(shape) - 2)


IN_TN = 1024
IN_QK_TILES = 2
IN_V_TILE = 2
IN_ZXG_TILE0 = 3
IN_MAIN_TILES = 11


def _inproj_kernel(x_ref, nw_ref, wm_ref, ws_ref, qk_ref, v_ref, zxg_ref, small_ref, xn_ref):
    j = pl.program_id(1)

    @pl.when(j == 0)
    def _():
        x = x_ref[...]
        ms = jnp.mean(x * x, axis=-1, keepdims=True)
        xn = (x * lax.rsqrt(ms + EPS)) * nw_ref[...]
        xn_ref[...] = xn.astype(BF16)
        small_ref[...] = _dot(xn_ref[...], ws_ref[...])

    @pl.when(j < IN_QK_TILES)
    def _():
        qk_ref[...] = _dot(xn_ref[...], wm_ref[...])

    @pl.when(j == IN_V_TILE)
    def _():
        v_ref[...] = _dot(xn_ref[...], wm_ref[...])

    @pl.when(j >= IN_ZXG_TILE0)
    def _():
        zxg_ref[...] = _dot(xn_ref[...], wm_ref[...])


def _inproj(x2d, nw, w_main, w_small, tm):
    rows = x2d.shape[0]
    grid = (rows // tm, IN_MAIN_TILES)
    n_zxg = IN_MAIN_TILES - IN_ZXG_TILE0
    return pl.pallas_call(
        _inproj_kernel,
        grid=grid,
        in_specs=[
            pl.BlockSpec((tm, D_MODEL), lambda i, j: (i, 0)),
            pl.BlockSpec((1, D_MODEL), lambda i, j: (0, 0)),
            pl.BlockSpec((D_MODEL, IN_TN), lambda i, j: (0, j)),
            pl.BlockSpec((D_MODEL, LANES), lambda i, j: (0, 0)),
        ],
        out_specs=[
            pl.BlockSpec((tm, IN_TN), lambda i, j: (i, jnp.minimum(j, IN_QK_TILES - 1))),
            pl.BlockSpec((tm, IN_TN), lambda i, j: (i, 0)),
            pl.BlockSpec((tm, IN_TN), lambda i, j: (i, jnp.clip(j - IN_ZXG_TILE0, 0, n_zxg - 1))),
            pl.BlockSpec((tm, LANES), lambda i, j: (i, 0)),
        ],
        out_shape=[
            jax.ShapeDtypeStruct((rows, IN_QK_TILES * IN_TN), F32),
            jax.ShapeDtypeStruct((rows, IN_TN), F32),
            jax.ShapeDtypeStruct((rows, n_zxg * IN_TN), F32),
            jax.ShapeDtypeStruct((rows, LANES), F32),
        ],
        scratch_shapes=[pltpu.VMEM((tm, D_MODEL), BF16)],
        compiler_params=pltpu.CompilerParams(
            dimension_semantics=("arbitrary", "arbitrary"), vmem_limit_bytes=VMEM_LIMIT),
        name="inproj",
    )(x2d, nw, w_main, w_small)


AUG0 = HEAD_DIM
AUG_SHIFT = AUG0 + 6
AUG_W = LANES // N_HEADS_ATT
CP_ONE = 3 * N_HEADS_ATT
CP_SHIFT = CP_ONE + 1
SHIFT_MAX = 45.0
SHIFT_MARGIN = 1.03
VT_ROWS = HEAD_DIM + 16


def _cum_logf(logf128, carry_ref, ltri_ref):
    tp = logf128.shape[0]
    c = carry_ref[0:1, :] + _dot3(ltri_ref[...], logf128)
    carry_ref[...] = jnp.broadcast_to(c[tp - 1:tp, :], carry_ref.shape)
    return c


def _c_parts(c, shift=None):
    hi, mid, lo = _split3(c)
    lane = _lane_iota(c.shape)
    h = N_HEADS_ATT
    tail = jnp.where(lane == CP_ONE, 1.0, 0.0)
    if shift is not None:
        tail = jnp.where(lane == CP_SHIFT, shift, tail)
    packed = jnp.where(
        lane < h, hi.astype(F32),
        jnp.where(lane < 2 * h, pltpu.roll(mid.astype(F32), h, 1),
                  jnp.where(lane < 3 * h, pltpu.roll(lo.astype(F32), 2 * h, 1), tail)))
    return packed.astype(BF16)


def _head_rows(x, extras, out_ref, scale, zero_tail):
    lane = _lane_iota((x.shape[0], LANES))
    for hp in range(N_HEADS_ATT // 2):
        pair = x[:, LANES * hp:LANES * (hp + 1)] * scale
        for o in range(2):
            h = 2 * hp + o
            xh = pair if o == 0 else pltpu.roll(pair, HEAD_DIM, 1)
            ex = pltpu.roll(extras, (AUG0 - AUG_W * h) % LANES, 1)
            if zero_tail:
                ex = jnp.where(lane < AUG0 + AUG_W, ex, 0.0)
            out_ref[0, h] = jnp.where(lane < HEAD_DIM, xh, ex).astype(BF16)


def _write_vt(v, vt_ref):
    vt = v.T
    tail_shape = (VT_ROWS - HEAD_DIM, v.shape[0])
    tail = jnp.where(_row_iota(tail_shape) == 0, 1.0, 0.0).astype(BF16)
    for h in range(N_HEADS_ATT):
        vt_ref[0, h, 0:HEAD_DIM, :] = vt[HEAD_DIM * h:HEAD_DIM * (h + 1), :].astype(BF16)
        vt_ref[0, h, HEAD_DIM:VT_ROWS, :] = tail


def _head_rmsnorm(x, g_ref, gt_ref, w):
    sq = x * x
    hi = sq.astype(BF16)
    lo = (sq - hi.astype(F32)).astype(BF16)
    ssq = _dot(hi, g_ref[...]) + _dot(lo, g_ref[...])
    rs = lax.rsqrt(ssq * (1.0 / HEAD_DIM) + EPS)
    rs_hi = rs.astype(BF16)
    rs_lo = (rs - rs_hi.astype(F32)).astype(BF16)
    rs_e = _dot(rs_hi, gt_ref[...]) + _dot(rs_lo, gt_ref[...])
    return (x * rs_e) * w


def _prep_new_kernel(q_ref, k_ref, v_ref, small_ref, c0_ref, shift_ref, bf_ref, qw_ref, kw_ref,
                     g_ref, gt_ref, selq_ref, selk_ref, ltri_ref, *rest, aliased):
    if aliased:
        rest = rest[2:]
    kout_ref, logf_ref, qaug_ref, kaug_ref, vt_ref, cend_ref, carry_ref = rest
    t = pl.program_id(1)

    @pl.when(t == 0)
    def _():
        carry_ref[...] = c0_ref[0]

    logf = _log_sigmoid(small_ref[...] + bf_ref[...])
    logf_ref[...] = logf[:, SMALL_F0:SMALL_F0 + N_HEADS_ATT]
    c = _cum_logf(logf, carry_ref, ltri_ref)
    cend_ref[0] = carry_ref[...]
    cp = _c_parts(c * LOG2E, shift_ref[0, 0:1, :])

    qn = _head_rmsnorm(q_ref[...], g_ref, gt_ref, qw_ref[...])
    _head_rows(qn, _dot(cp, selq_ref[...]), qaug_ref, ATT_SCALE * LOG2E, False)
    kn = _head_rmsnorm(k_ref[...], g_ref, gt_ref, kw_ref[...])
    _head_rows(kn, _dot(cp, selk_ref[...]), kaug_ref, 1.0, True)
    kout_ref[...] = kn
    _write_vt(v_ref[...], vt_ref)


def _prep_cache_kernel(k_ref, v_ref, logf_ref, c0_ref, g_ref, selk_ref, ltri_ref,
                       kaug_ref, vt_ref, cend_ref, kmax_ref, carry_ref):
    t = pl.program_id(1)

    @pl.when(t == 0)
    def _():
        carry_ref[...] = c0_ref[0]
        kmax_ref[...] = jnp.zeros(kmax_ref.shape, F32)

    c = _cum_logf(logf_ref[0], carry_ref, ltri_ref)
    cend_ref[0] = carry_ref[...]
    k = k_ref[0]
    _head_rows(k, _dot(_c_parts(c * LOG2E), selk_ref[...]), kaug_ref, 1.0, True)
    _write_vt(v_ref[0], vt_ref)
    sq = k * k
    hi = sq.astype(BF16)
    lo = (sq - hi.astype(F32)).astype(BF16)
    ssq = _dot(hi, g_ref[...]) + _dot(lo, g_ref[...])
    kmax_ref[0] = jnp.maximum(kmax_ref[0], jnp.max(ssq, axis=0, keepdims=True))


def _aug_selectors():
    h = N_HEADS_ATT
    rows = jnp.arange(LANES)[:, None]
    cols = jnp.arange(LANES)[None, :]
    head = cols // AUG_W
    pos = cols % AUG_W
    one_row = rows == CP_ONE
    selq = jnp.zeros((LANES, LANES), F32)
    selk = jnp.zeros((LANES, LANES), F32)
    for part in range(3):
        part_row = rows == part * h + head
        selq = selq + jnp.where(part_row & (pos == part), 1.0, 0.0)
        selq = selq + jnp.where(one_row & (pos == 3 + part), 1.0, 0.0)
        selk = selk + jnp.where(one_row & (pos == part), 1.0, 0.0)
        selk = selk - jnp.where(part_row & (pos == 3 + part), 1.0, 0.0)
    selq = selq - jnp.where((rows == CP_SHIFT) & (pos == AUG_SHIFT - AUG0), 1.0, 0.0)
    selk = selk + jnp.where(one_row & (pos == AUG_SHIFT - AUG0), 1.0, 0.0)
    return selq.astype(BF16), selk.astype(BF16)


def _head_group_matrices():
    j = jnp.arange(ATT_WIDTH)[:, None] // HEAD_DIM
    h = jnp.arange(LANES)[None, :]
    g = jnp.where(j == h, 1.0, 0.0).astype(BF16)
    return g, g.T


def _ltri(n, block=None):
    r = jnp.arange(n)[:, None]
    c = jnp.arange(n)[None, :]
    keep = c <= r
    if block is not None:
        keep = keep & (r // block == c // block)
    return jnp.where(keep, 1.0, 0.0).astype(BF16)


def _prep_new(qk, v, small, c0, shift, bf128, qw, kw, batch, seq, tk, bufs=None, t_off_blocks=0):
    nt = seq // tk
    h = N_HEADS_ATT
    g, gt = _head_group_matrices()
    selq, selk = _aug_selectors()
    ltri = _ltri(tk)
    rows = batch * seq
    const = lambda shape: pl.BlockSpec(shape, lambda b, t: tuple(0 for _ in shape))
    in_specs = [
        pl.BlockSpec((tk, ATT_WIDTH), lambda b, t: (b * nt + t, 0)),
        pl.BlockSpec((tk, ATT_WIDTH), lambda b, t: (b * nt + t, 1)),
        pl.BlockSpec((tk, ATT_WIDTH), lambda b, t: (b * nt + t, 0)),
        pl.BlockSpec((tk, LANES), lambda b, t: (b * nt + t, 0)),
        pl.BlockSpec((1, SUBLANES, LANES), lambda b, t: (b, 0, 0)),
        pl.BlockSpec((1, SUBLANES, LANES), lambda b, t: (b, 0, 0)),
        const((1, LANES)), const((1, ATT_WIDTH)), const((1, ATT_WIDTH)),
        const((ATT_WIDTH, LANES)), const((LANES, ATT_WIDTH)),
        const((LANES, LANES)), const((LANES, LANES)), const((tk, tk)),
    ]
    args = [qk, qk, v, small, c0, shift, bf128, qw, kw, g, gt, selq, selk, ltri]
    if bufs is None:
        kv_len = seq
        aliases = {}
    else:
        kv_len = bufs[0].shape[2]
        in_specs += [pl.BlockSpec(memory_space=pl.ANY), pl.BlockSpec(memory_space=pl.ANY)]
        aliases = {len(args): 3, len(args) + 1: 4}
        args += list(bufs)
    o = t_off_blocks
    out_specs = [
        pl.BlockSpec((tk, ATT_WIDTH), lambda b, t: (b * nt + t, 0)),
        pl.BlockSpec((tk, h), lambda b, t: (b * nt + t, 0)),
        pl.BlockSpec((1, h, tk, LANES), lambda b, t: (b, 0, t, 0)),
        pl.BlockSpec((1, h, tk, LANES), lambda b, t: (b, 0, t + o, 0)),
        pl.BlockSpec((1, h, VT_ROWS, tk), lambda b, t: (b, 0, 0, t + o)),
        pl.BlockSpec((1, SUBLANES, LANES), lambda b, t: (b, 0, 0)),
    ]
    out_shape = [
        jax.ShapeDtypeStruct((rows, ATT_WIDTH), F32),
        jax.ShapeDtypeStruct((rows, h), F32),
        jax.ShapeDtypeStruct((batch, h, seq, LANES), BF16),
        jax.ShapeDtypeStruct((batch, h, kv_len, LANES), BF16),
        jax.ShapeDtypeStruct((batch, h, VT_ROWS, kv_len), BF16),
        jax.ShapeDtypeStruct((batch, SUBLANES, LANES), F32),
    ]
    return pl.pallas_call(
        functools.partial(_prep_new_kernel, aliased=bufs is not None),
        grid=(batch, nt),
        in_specs=in_specs,
        out_specs=out_specs,
        out_shape=out_shape,
        scratch_shapes=[pltpu.VMEM((SUBLANES, LANES), F32)],
        input_output_aliases=aliases,
        compiler_params=pltpu.CompilerParams(
            dimension_semantics=("arbitrary", "arbitrary"), vmem_limit_bytes=VMEM_LIMIT),
        name="prep_new",
    )(*args)


def _prep_cache(cache_k, cache_v, cache_logf128, c0, tk, kv_len):
    batch, past, _ = cache_k.shape
    nt = past // tk
    h = N_HEADS_ATT
    g, _ = _head_group_matrices()
    _, selk = _aug_selectors()
    ltri = _ltri(tk)
    const = lambda shape: pl.BlockSpec(shape, lambda b, t: tuple(0 for _ in shape))
    return pl.pallas_call(
        _prep_cache_kernel,
        grid=(batch, nt),
        in_specs=[
            pl.BlockSpec((1, tk, ATT_WIDTH), lambda b, t: (b, t, 0)),
            pl.BlockSpec((1, tk, ATT_WIDTH), lambda b, t: (b, t, 0)),
            pl.BlockSpec((1, tk, LANES), lambda b, t: (b, t, 0)),
            pl.BlockSpec((1, SUBLANES, LANES), lambda b, t: (b, 0, 0)),
            const((ATT_WIDTH, LANES)), const((LANES, LANES)), const((tk, tk)),
        ],
        out_specs=[
            pl.BlockSpec((1, h, tk, LANES), lambda b, t: (b, 0, t, 0)),
            pl.BlockSpec((1, h, VT_ROWS, tk), lambda b, t: (b, 0, 0, t)),
            pl.BlockSpec((1, SUBLANES, LANES), lambda b, t: (b, 0, 0)),
            pl.BlockSpec((1, 1, LANES), lambda b, t: (b, 0, 0)),
        ],
        out_shape=[
            jax.ShapeDtypeStruct((batch, h, kv_len, LANES), BF16),
            jax.ShapeDtypeStruct((batch, h, VT_ROWS, kv_len), BF16),
            jax.ShapeDtypeStruct((batch, SUBLANES, LANES), F32),
            jax.ShapeDtypeStruct((batch, 1, LANES), F32),
        ],
        scratch_shapes=[pltpu.VMEM((SUBLANES, LANES), F32)],
        compiler_params=pltpu.CompilerParams(
            dimension_semantics=("arbitrary", "arbitrary"), vmem_limit_bytes=VMEM_LIMIT),
        name="prep_cache",
    )(cache_k, cache_v, cache_logf128, c0, g, selk, ltri)


def _flash_kernel(q_ref, k_ref, vt_ref, o_ref, s0_ref, s1_ref, cm_ref, m_ref, acc_ref,
                  *, tq, tkf, q_off):
    qi = pl.program_id(2)
    q_start = q_off + qi * tq
    n_full = q_start // tkf
    acc_ref[...] = jnp.zeros(acc_ref.shape, F32)
    m_ref[...] = jnp.full(m_ref.shape, NEG_BIG, F32)
    bufs = (s0_ref, s1_ref)

    def scores(k_start, rows, masked, slot):
        for o in range(2):
            kc = k_ref[0, o, pl.ds(k_start, rows), :]
            st = lax.dot_general(kc, q_ref[0, o], (((1,), (1,)), ((), ())),
                                 preferred_element_type=F32)
            if masked:
                st = jnp.where(_row_iota(st.shape) <= _lane_iota(st.shape), st, NEG_BIG)
            bufs[slot][o, 0:rows, :] = st
            cm_ref[slot, o] = jnp.max(st, axis=0, keepdims=True)

    def consume(k_start, rows, slot):
        for o in range(2):
            m_prev = m_ref[o]
            m_new = jnp.maximum(m_prev, cm_ref[slot, o])
            alpha = jnp.exp2(m_prev - m_new)
            p = jnp.exp2(bufs[slot][o, 0:rows, :] - m_new)
            pv = _dot(vt_ref[0, o, :, pl.ds(k_start, rows)], p.astype(BF16))
            acc_ref[o] = alpha * acc_ref[o] + pv
            m_ref[o] = m_new

    full_start = lambda f: pl.multiple_of(f * tkf, tkf)
    diag_start = pl.multiple_of(q_start, tq)
    scores(diag_start, tq, True, 0)

    @pl.when(n_full > 0)
    def _():
        scores(full_start(0), tkf, False, 1)

    consume(diag_start, tq, 0)

    pairs = jnp.maximum(n_full - 1, 0) // 2

    def body(t, carry):
        f = 2 * t
        scores(full_start(f + 1), tkf, False, 0)
        consume(full_start(f), tkf, 1)
        scores(full_start(f + 2), tkf, False, 1)
        consume(full_start(f + 1), tkf, 0)
        return carry

    lax.fori_loop(0, pairs, body, 0)
    f_rem = 2 * pairs
    left = n_full - f_rem

    @pl.when(left == 2)
    def _():
        scores(full_start(f_rem + 1), tkf, False, 0)
        consume(full_start(f_rem), tkf, 1)
        consume(full_start(f_rem + 1), tkf, 0)

    @pl.when(left == 1)
    def _():
        consume(full_start(f_rem), tkf, 1)

    outs = [acc_ref[o, 0:HEAD_DIM, :] / acc_ref[o, HEAD_DIM:HEAD_DIM + 1, :] for o in range(2)]
    o_ref[0] = jnp.concatenate(outs, axis=0).T.astype(BF16)


FLASH_UNROLL = 4


def _flash_fixed_kernel(q_ref, k_ref, vt_ref, o_ref, pa_ref, pb_ref, acc_ref, *, tq, tkf, q_off):
    qi = pl.program_id(2)
    q_start = q_off + qi * tq
    n_full = q_start // tkf
    slot_of = lambda f_parity: pa_ref if f_parity else pb_ref

    def probs(k_start, rows, masked, p_ref):
        for o in range(2):
            kc = k_ref[0, o, pl.ds(k_start, rows), :]
            st = lax.dot_general(kc, q_ref[0, o], (((1,), (1,)), ((), ())),
                                 preferred_element_type=F32)
            if masked:
                st = jnp.where(_row_iota(st.shape) <= _lane_iota(st.shape), st, NEG_BIG)
            p_ref[o, 0:rows, :] = jnp.exp2(st).astype(BF16)

    def accumulate(k_start, rows, p_ref, first=False):
        for o in range(2):
            pv = _dot(vt_ref[0, o, :, pl.ds(k_start, rows)], p_ref[o, 0:rows, :])
            acc_ref[o] = pv if first else acc_ref[o] + pv

    full_start = lambda f: pl.multiple_of(f * tkf, tkf)
    diag_start = pl.multiple_of(q_start, tq)
    probs(diag_start, tq, True, pa_ref)

    if tq == tkf:
        start_of = lambda g: pl.multiple_of(jnp.where(g == 0, q_start, (g - 1) * tkf), tkf)
        acc_ref[...] = jnp.zeros(acc_ref.shape, F32)
        n_steps = n_full
        enabled = True

        def step(g, parity):
            probs(full_start(g), tkf, False, slot_of(parity))
            accumulate(start_of(g), tkf, slot_of(1 - parity))

        last = lambda g, parity: accumulate(start_of(g), tkf, slot_of(1 - parity))
    else:
        @pl.when(n_full > 0)
        def _():
            probs(full_start(0), tkf, False, slot_of(0))

        accumulate(diag_start, tq, pa_ref, first=True)
        n_steps = jnp.maximum(n_full - 1, 0)
        enabled = n_full > 0

        def step(f, parity):
            probs(full_start(f + 1), tkf, False, slot_of(1 - parity))
            accumulate(full_start(f), tkf, slot_of(parity))

        last = lambda f, parity: accumulate(full_start(f), tkf, slot_of(parity))

    def body(t, carry):
        for i in range(FLASH_UNROLL):
            step(FLASH_UNROLL * t + i, i % 2)
        return carry

    lax.fori_loop(0, n_steps // FLASH_UNROLL, body, 0)
    f0 = (n_steps // FLASH_UNROLL) * FLASH_UNROLL
    left = n_steps - f0
    for r in range(FLASH_UNROLL):
        @pl.when(enabled & (left == r))
        def _(r=r):
            for i in range(r):
                step(f0 + i, i % 2)
            last(f0 + r, r % 2)

    outs = [acc_ref[o, 0:HEAD_DIM, :] / acc_ref[o, HEAD_DIM:HEAD_DIM + 1, :] for o in range(2)]
    o_ref[0] = jnp.concatenate(outs, axis=0).T.astype(BF16)


def _flash(q_aug, k_aug, vt, tq, tkf, q_off, fixed_reference):
    batch, h, tq_total, _ = q_aug.shape
    kv_len = k_aug.shape[2]
    nq = tq_total // tq
    if fixed_reference:
        body = _flash_fixed_kernel
        scratch = [
            pltpu.VMEM((2, max(tq, tkf), tq), BF16),
            pltpu.VMEM((2, max(tq, tkf), tq), BF16),
            pltpu.VMEM((2, VT_ROWS, tq), F32),
        ]
    else:
        body = _flash_kernel
        scratch = [
            pltpu.VMEM((2, max(tq, tkf), tq), F32),
            pltpu.VMEM((2, max(tq, tkf), tq), F32),
            pltpu.VMEM((2, 2, 1, tq), F32),
            pltpu.VMEM((2, 1, tq), F32),
            pltpu.VMEM((2, VT_ROWS, tq), F32),
        ]
    return pl.pallas_call(
        functools.partial(body, tq=tq, tkf=tkf, q_off=q_off),
        grid=(batch, h // 2, nq),
        in_specs=[
            pl.BlockSpec((1, 2, tq, LANES), lambda b, hp, qi: (b, hp, qi, 0)),
            pl.BlockSpec((1, 2, kv_len, LANES), lambda b, hp, qi: (b, hp, 0, 0)),
            pl.BlockSpec((1, 2, VT_ROWS, kv_len), lambda b, hp, qi: (b, hp, 0, 0)),
        ],
        out_specs=pl.BlockSpec((1, tq, LANES), lambda b, hp, qi: (b, qi, hp)),
        out_shape=jax.ShapeDtypeStruct((batch, tq_total, ATT_WIDTH), BF16),
        scratch_shapes=scratch,
        compiler_params=pltpu.CompilerParams(
            dimension_semantics=("arbitrary", "arbitrary", "arbitrary"),
            vmem_limit_bytes=VMEM_LIMIT),
        name="flash_fixed" if fixed_reference else "flash",
    )(q_aug, k_aug, vt)


B_LANE0 = D_INNER
C_LANE0 = D_INNER + N_GROUPS * D_STATE
PK_HI0 = SMALL_DT0
PK_MID0 = PK_HI0 + N_HEADS_SSM
PK_LO0 = PK_MID0 + N_HEADS_SSM


def _pack_head_parts(x):
    hi, mid, lo = _split3(x)
    lane = _lane_iota(x.shape)
    packed = jnp.where(
        lane < PK_MID0, hi.astype(F32),
        jnp.where(lane < PK_LO0, pltpu.roll(mid.astype(F32), N_HEADS_SSM, 1),
                  pltpu.roll(lo.astype(F32), 2 * N_HEADS_SSM, 1)))
    return packed.astype(BF16)


def _head_expand_matrix():
    rows = jnp.arange(LANES)[:, None]
    head = jnp.arange(D_INNER)[None, :] // SSM_HEAD_DIM
    hit = (rows == PK_HI0 + head) | (rows == PK_MID0 + head) | (rows == PK_LO0 + head)
    return jnp.where(hit, 1.0, 0.0).astype(BF16)


def _ssd_kernel(z_ref, xs_ref, bc_ref, small_ref, cpast_ref, h0_ref, cw_ref, cb_ref,
                dtb_ref, alog_ref, dskip_ref, nw_ref, lblk_ref, sel_ref,
                y_ref, cnew_ref, hlast_ref,
                xbuf_ref, ht_ref, ae_ref, dte_ref, ybuf_ref, *, tb, n_valid, tail_row):
    t = pl.program_id(1)
    nt = pl.num_programs(1)
    n_state_blocks = D_INNER // LANES

    n_conv_slabs = CONV_DIM // LANES
    n_inner_slabs = D_INNER // LANES

    @pl.when(t == 0)
    def _():
        for j in range(n_conv_slabs):
            xbuf_ref[j, 0:SUBLANES, :] = cpast_ref[0, :, LANES * j:LANES * (j + 1)]
        for i in range(n_state_blocks):
            ht_ref[:, LANES * i:LANES * (i + 1)] = h0_ref[0, LANES * i:LANES * (i + 1), :].T

    for j in range(n_inner_slabs):
        xbuf_ref[j, SUBLANES:SUBLANES + tb, :] = xs_ref[:, LANES * j:LANES * (j + 1)]
        xbuf_ref[n_inner_slabs + j, SUBLANES:SUBLANES + tb, :] = bc_ref[:, LANES * j:LANES * (j + 1)]

    lane = _lane_iota((tb, LANES))
    row = _row_iota((tb, LANES)) + t * tb
    live = (lane >= SMALL_DT0) & (lane < SMALL_DT0 + N_HEADS_SSM) & (row < n_valid)
    dt = jnp.where(live, _softplus(small_ref[...] + dtb_ref[...]), 0.0)
    a = dt * (-jnp.exp(alog_ref[...]))
    acum = _dot3(lblk_ref[...], a)
    ae_all = _dot(_pack_head_parts(acum), sel_ref[...])
    dte_all = _dot(_pack_head_parts(dt), sel_ref[...])
    for j in range(n_inner_slabs):
        ae_ref[j] = ae_all[:, LANES * j:LANES * (j + 1)]
        dte_ref[j] = dte_all[:, LANES * j:LANES * (j + 1)]

    gw = GROUP_WIDTH
    token_of = lambda r: 32 * (r // 32) + (r // SUBLANES) % 4 + 4 * (r % SUBLANES)
    sub = _row_iota((SSD_CHUNK, gw))
    pos = _lane_iota((SSD_CHUNK, gw)) % SSM_HEAD_DIM
    diag = pos == sub
    causal = token_of(pos) <= token_of(sub)
    blk_row = _row_iota((gw, gw)) // SSM_HEAD_DIM
    blk_col = _lane_iota((gw, gw)) // SSM_HEAD_DIM
    same_head = blk_row == blk_col
    windows = [(32 * half + a) for half in range(2) for a in range(4)]

    def load_perm(ref, j, row0):
        return jnp.concatenate(
            [ref[j, pl.ds(row0 + w, SUBLANES, stride=4), :] for w in windows], axis=0)

    def conv_slab(c0, j):
        cols = slice(LANES * j, LANES * (j + 1))
        out = cb_ref[:, cols]
        for i in range(CONV_W):
            out = out + load_perm(xbuf_ref, j, c0 + SUBLANES - (CONV_W - 1) + i) * cw_ref[i:i + 1, cols]
        return _silu(out)

    def chunk(c, carry):
        c0 = pl.multiple_of(c * SSD_CHUNK, SSD_CHUNK)
        rows = pl.ds(c0, SSD_CHUNK)
        for g in range(N_GROUPS):
            lanes = slice(gw * g, gw * (g + 1))
            j0 = 2 * g
            xs = jnp.concatenate([conv_slab(c0, j0), conv_slab(c0, j0 + 1)], axis=1)
            bm = conv_slab(c0, n_inner_slabs + g)
            cm = conv_slab(c0, n_inner_slabs + N_GROUPS + g)
            ae = jnp.concatenate([load_perm(ae_ref, j0, c0), load_perm(ae_ref, j0 + 1, c0)], axis=1)
            dte = jnp.concatenate([load_perm(dte_ref, j0, c0), load_perm(dte_ref, j0 + 1, c0)], axis=1)
            xdt = xs * dte
            bm16 = bm.astype(BF16)
            cm16 = cm.astype(BF16)

            b4 = jnp.concatenate([bm16] * HEADS_PER_GROUP, axis=0)
            cb4 = lax.dot_general(cm16, b4, (((1,), (1,)), ((), ())), preferred_element_type=F32)
            a_row = jnp.sum(jnp.where(diag, ae, 0.0), axis=0, keepdims=True)
            decay = jnp.exp(jnp.where(causal, ae - a_row, NEG_BIG))
            xdt16 = xdt.astype(BF16)
            x4 = jnp.where(same_head, jnp.concatenate([xdt16] * HEADS_PER_GROUP, axis=0),
                           jnp.zeros((), BF16))
            y = _dot((cb4 * decay).astype(BF16), x4)

            hprev = ht_ref[:, lanes]
            y = y + jnp.exp(ae) * _dot(cm16, hprev.astype(BF16))
            a_last = ae[SSD_CHUNK - 1:SSD_CHUNK, :]
            xw = (xdt * jnp.exp(a_last - ae)).astype(BF16)
            upd = lax.dot_general(bm16, xw, (((0,), (0,)), ((), ())), preferred_element_type=F32)
            ht_ref[:, lanes] = hprev * jnp.exp(a_last) + upd

            y = y + xs * dskip_ref[:, lanes]
            for jj in range(2):
                for v, w in enumerate(windows):
                    ybuf_ref[jj, pl.ds(c0 + w, SUBLANES, stride=4), :] = (
                        y[SUBLANES * v:SUBLANES * (v + 1), LANES * jj:LANES * (jj + 1)])
            y = jnp.concatenate([ybuf_ref[0, rows, :], ybuf_ref[1, rows, :]], axis=1)
            y = y * _silu(z_ref[rows, lanes])
            ms = jnp.mean(y * y, axis=-1, keepdims=True)
            y_ref[rows, lanes] = ((y * lax.rsqrt(ms + EPS)) * nw_ref[:, lanes]).astype(BF16)
        return carry

    lax.fori_loop(0, tb // SSD_CHUNK, chunk, 0)

    for j in range(n_conv_slabs):
        cnew_ref[0, :, LANES * j:LANES * (j + 1)] = xbuf_ref[j, tail_row:tail_row + SUBLANES, :]
        xbuf_ref[j, 0:SUBLANES, :] = xbuf_ref[j, tb:tb + SUBLANES, :]

    @pl.when(t == nt - 1)
    def _():
        for i in range(n_state_blocks):
            hlast_ref[0, LANES * i:LANES * (i + 1), :] = ht_ref[:, LANES * i:LANES * (i + 1)].T


def _ssd(zxg, small, conv_past8, h0, cw, cb, dtb128, alog128, dskip_e, nw, batch, seq, tb, n_valid):
    nt = seq // tb
    rows = batch * seq
    tail_row = n_valid - (seq - tb)
    lblk = _ltri(tb, SSD_CHUNK)
    sel = _head_expand_matrix()
    const = lambda shape: pl.BlockSpec(shape, lambda b, t: tuple(0 for _ in shape))
    return pl.pallas_call(
        functools.partial(_ssd_kernel, tb=tb, n_valid=n_valid, tail_row=tail_row),
        grid=(batch, nt),
        in_specs=[
            pl.BlockSpec((tb, D_INNER), lambda b, t: (b * nt + t, 0)),
            pl.BlockSpec((tb, D_INNER), lambda b, t: (b * nt + t, 1)),
            pl.BlockSpec((tb, D_INNER), lambda b, t: (b * nt + t, 2)),
            pl.BlockSpec((tb, LANES), lambda b, t: (b * nt + t, 0)),
            pl.BlockSpec((1, SUBLANES, CONV_DIM), lambda b, t: (b, 0, 0)),
            pl.BlockSpec((1, D_INNER, D_STATE), lambda b, t: (b, 0, 0)),
            const((CONV_W, CONV_DIM)), const((1, CONV_DIM)),
            const((1, LANES)), const((1, LANES)), const((1, D_INNER)), const((1, D_INNER)),
            const((tb, tb)), const((LANES, D_INNER)),
        ],
        out_specs=[
            pl.BlockSpec((tb, D_INNER), lambda b, t: (b * nt + t, 0)),
            pl.BlockSpec((1, SUBLANES, CONV_DIM), lambda b, t: (b, 0, 0)),
            pl.BlockSpec((1, D_INNER, D_STATE), lambda b, t: (b, 0, 0)),
        ],
        out_shape=[
            jax.ShapeDtypeStruct((rows, D_INNER), BF16),
            jax.ShapeDtypeStruct((batch, SUBLANES, CONV_DIM), F32),
            jax.ShapeDtypeStruct((batch, D_INNER, D_STATE), F32),
        ],
        scratch_shapes=[
            pltpu.VMEM((CONV_DIM // LANES, tb + SUBLANES, LANES), F32),
            pltpu.VMEM((D_STATE, D_INNER), F32),
            pltpu.VMEM((D_INNER // LANES, tb, LANES), F32),
            pltpu.VMEM((D_INNER // LANES, tb, LANES), F32),
            pltpu.VMEM((2, tb, LANES), F32),
        ],
        compiler_params=pltpu.CompilerParams(
            dimension_semantics=("arbitrary", "arbitrary"), vmem_limit_bytes=VMEM_LIMIT),
        name="ssd",
    )(zxg, zxg, zxg, small, conv_past8, h0, cw, cb, dtb128, alog128, dskip_e, nw, lblk, sel)


FF_CHUNK = 1024


def _merge_kernel(att_ref, ys_ref, ga_ref, gs_ref, x_ref, wpa_ref, wpb_ref, wout_ref,
                  n2_ref, wup_ref, wdn_ref, o_ref):
    merged = (_sigmoid(ga_ref[...]) * _dot(att_ref[...], wpa_ref[...])
              + _sigmoid(gs_ref[...]) * _dot(ys_ref[...], wpb_ref[...]))
    x1 = x_ref[...] + _dot(merged.astype(BF16), wout_ref[...])
    ms = jnp.mean(x1 * x1, axis=-1, keepdims=True)
    hn = ((x1 * lax.rsqrt(ms + EPS)) * n2_ref[...]).astype(BF16)
    acc = x1
    for c in range(D_FF // FF_CHUNK):
        cols = slice(FF_CHUNK * c, FF_CHUNK * (c + 1))
        up = jnp.maximum(_dot(hn, wup_ref[:, cols]), 0.0)
        acc = acc + _dot((up * up).astype(BF16), wdn_ref[cols, :])
    o_ref[...] = acc


def _merge(att, ys, zxg, x2d, wpa, wpb, wout, n2, wup, wdn, tm):
    rows = x2d.shape[0]
    ga_blk = 6
    gs_blk = 7
    resident = lambda shape: pl.BlockSpec(shape, lambda i: tuple(0 for _ in shape),
                                          pipeline_mode=pl.Buffered(1))
    return pl.pallas_call(
        _merge_kernel,
        grid=(rows // tm,),
        in_specs=[
            pl.BlockSpec((tm, ATT_WIDTH), lambda i: (i, 0)),
            pl.BlockSpec((tm, D_INNER), lambda i: (i, 0)),
            pl.BlockSpec((tm, D_MODEL), lambda i: (i, ga_blk)),
            pl.BlockSpec((tm, D_MODEL), lambda i: (i, gs_blk)),
            pl.BlockSpec((tm, D_MODEL), lambda i: (i, 0)),
            resident((ATT_WIDTH, D_MODEL)), resident((D_INNER, D_MODEL)),
            resident((D_MODEL, D_MODEL)), resident((1, D_MODEL)),
            resident((D_MODEL, D_FF)), resident((D_FF, D_MODEL)),
        ],
        out_specs=pl.BlockSpec((tm, D_MODEL), lambda i: (i, 0)),
        out_shape=jax.ShapeDtypeStruct((rows, D_MODEL), F32),
        compiler_params=pltpu.CompilerParams(
            dimension_semantics=("arbitrary",), vmem_limit_bytes=VMEM_LIMIT),
        name="merge",
    )(att, ys, zxg, zxg, x2d, wpa, wpb, wout, n2, wup, wdn)


def _pick_tile(n, prefs):
    for p in prefs:
        if n % p == 0:
            return p
    raise ValueError(f"no tile in {prefs} divides {n}")


def _pad_lanes(vec, lane0):
    return jnp.zeros((1, LANES), F32).at[0, lane0:lane0 + vec.shape[0]].set(vec.astype(F32))


def _layer(x, past, p, n_valid):
    batch, seq, _ = x.shape
    rows = batch * seq
    x2d = x.reshape(rows, D_MODEL)

    tm = _pick_tile(rows, (1024, 512, 256, 128))
    qk, v, zxg, small = _inproj(x2d, p["norm1_w"], p["w_main"], p["w_small"], tm)

    tk = _pick_tile(seq, (512, 256, 128)) if past is None else LANES
    zeros_c = jnp.zeros((batch, SUBLANES, LANES), F32)
    k2_new = HEAD_DIM * jnp.max(jnp.square(p["kw"]))
    if past is None:
        k2 = jnp.full((batch,), k2_new, F32)
    else:
        cache_k, cache_v, cache_logf, conv_past, h0 = past
        n_past = cache_k.shape[1]
        kv_len = n_past + seq
        logf128 = jnp.pad(cache_logf.astype(F32), ((0, 0), (0, 0), (0, LANES - N_HEADS_ATT)))
        tkf = _pick_tile(n_past, (1024, 512, 256, 128))
        k_buf, vt_buf, c_past, k2_cache = _prep_cache(cache_k, cache_v, logf128, zeros_c, tkf, kv_len)
        k2 = jnp.maximum(k2_new, jnp.max(k2_cache, axis=(1, 2)))
    q_norm = jnp.sqrt(HEAD_DIM * jnp.max(jnp.square(p["qw"])))
    shift = (SHIFT_MARGIN * ATT_SCALE * LOG2E) * q_norm * jnp.sqrt(k2)
    shift = shift.astype(BF16).astype(F32)
    shift_rows = jnp.broadcast_to(shift[:, None, None], (batch, SUBLANES, LANES))
    if past is None:
        k_out, logf, q_aug, k_aug, vt, _ = _prep_new(
            qk, v, small, zeros_c, shift_rows, p["bf128"], p["qw"], p["kw"], batch, seq, tk)
        tkf, q_off = tk, 0
        conv_past8 = jnp.zeros((batch, SUBLANES, CONV_DIM), F32)
        h0 = jnp.zeros((batch, D_INNER, D_STATE), F32)
    else:
        k_out, logf, q_aug, k_aug, vt, _ = _prep_new(
            qk, v, small, c_past, shift_rows, p["bf128"], p["qw"], p["kw"], batch, seq, tk,
            bufs=(k_buf, vt_buf), t_off_blocks=n_past // tk)
        q_off = n_past
        conv_past8 = jnp.pad(conv_past.astype(F32), ((0, 0), (SUBLANES - (CONV_W - 1), 0), (0, 0)))
        h0 = h0.astype(F32).reshape(batch, D_INNER, D_STATE)

    att = lax.cond(jnp.max(shift) <= SHIFT_MAX,
                   lambda: _flash(q_aug, k_aug, vt, tk, tkf, q_off, True),
                   lambda: _flash(q_aug, k_aug, vt, tk, tkf, q_off, False))

    tb = _pick_tile(seq, (256, 128))
    ys, conv_new8, h_last = _ssd(zxg, small, conv_past8, h0, p["conv_w"], p["conv_b"],
                                 p["dtb128"], p["alog128"], p["dskip_e"], p["ssm_norm_w"],
                                 batch, seq, tb, n_valid)

    tmm = _pick_tile(rows, (512, 256, 128))
    y = _merge(att.reshape(rows, ATT_WIDTH), ys, zxg, x2d, p["w_pa"], p["w_pb"], p["w_out"],
               p["norm2_w"], p["w_up"], p["w_down"], tmm)

    nv = n_valid
    y = y.reshape(batch, seq, D_MODEL)[:, :nv]
    k_new = k_out.reshape(batch, seq, N_HEADS_ATT, HEAD_DIM)[:, :nv]
    v_new = v.reshape(batch, seq, N_HEADS_ATT, HEAD_DIM)[:, :nv]
    logf_new = logf.reshape(batch, seq, N_HEADS_ATT)[:, :nv]
    conv_new = conv_new8[:, SUBLANES - (CONV_W - 1):, :]
    h_new = h_last.reshape(batch, N_HEADS_SSM, SSM_HEAD_DIM, D_STATE)
    return y, (k_new, v_new, logf_new, conv_new, h_new)


def _layer_params(l, norm1_w, w_in, b_f, q_norm_w, k_norm_w, conv_w, conv_b, dt_bias, A_log,
                  D_skip, ssm_norm_w, w_pa, w_pb, w_out, norm2_w, w_up, w_down):
    w = w_in[l]
    o = 0
    seg = {}
    for name, size in (("q", ATT_WIDTH), ("k", ATT_WIDTH), ("v", ATT_WIDTH), ("f", N_HEADS_ATT),
                       ("z", D_INNER), ("xbc", CONV_DIM), ("dt", N_HEADS_SSM),
                       ("ga", D_MODEL), ("gs", D_MODEL)):
        seg[name] = w[:, o:o + size]
        o += size
    w_main = jnp.concatenate([seg[n] for n in ("q", "k", "v", "z", "xbc", "ga", "gs")], axis=1)
    w_small = jnp.zeros((D_MODEL, LANES), w.dtype)
    w_small = w_small.at[:, SMALL_F0:SMALL_F0 + N_HEADS_ATT].set(seg["f"])
    w_small = w_small.at[:, SMALL_DT0:SMALL_DT0 + N_HEADS_SSM].set(seg["dt"])
    row = lambda a: a.astype(F32).reshape(1, -1)
    return {
        "norm1_w": row(norm1_w[l]),
        "w_main": w_main.astype(BF16),
        "w_small": w_small.astype(BF16),
        "bf128": _pad_lanes(b_f[l], SMALL_F0),
        "qw": row(jnp.tile(q_norm_w[l], N_HEADS_ATT)),
        "kw": row(jnp.tile(k_norm_w[l], N_HEADS_ATT)),
        "conv_w": conv_w[l].astype(F32),
        "conv_b": row(conv_b[l]),
        "dtb128": _pad_lanes(dt_bias[l], SMALL_DT0),
        "alog128": _pad_lanes(A_log[l], SMALL_DT0),
        "dskip_e": row(jnp.repeat(D_skip[l], SSM_HEAD_DIM)),
        "ssm_norm_w": row(ssm_norm_w[l]),
        "w_pa": w_pa[l].astype(BF16),
        "w_pb": w_pb[l].astype(BF16),
        "w_out": w_out[l].astype(BF16),
        "norm2_w": row(norm2_w[l]),
        "w_up": w_up[l].astype(BF16),
        "w_down": w_down[l].astype(BF16),
    }


def kernel(x_prompt, x_sample, cache_k, cache_v, cache_logf, state_conv, state_ssm, norm1_w, w_in,
           b_f, q_norm_w, k_norm_w, conv_w, conv_b, dt_bias, A_log, D_skip, ssm_norm_w, w_pa, w_pb,
           w_out, norm2_w, w_up, w_down):
    depth = w_in.shape[0]
    dec_batch, dec_seq, _ = x_sample.shape
    dec_pad = -(-dec_seq // LANES) * LANES
    yp = x_prompt
    ys = jnp.pad(x_sample, ((0, 0), (0, dec_pad - dec_seq), (0, 0)))
    states_p, states_s = [], []
    for l in range(depth):
        p = _layer_params(l, norm1_w, w_in, b_f, q_norm_w, k_norm_w, conv_w, conv_b, dt_bias,
                          A_log, D_skip, ssm_norm_w, w_pa, w_pb, w_out, norm2_w, w_up, w_down)
        yp, st_p = _layer(yp, None, p, yp.shape[1])
        n_past = cache_k.shape[2]
        past = (cache_k[l].reshape(dec_batch, n_past, ATT_WIDTH),
                cache_v[l].reshape(dec_batch, n_past, ATT_WIDTH),
                cache_logf[l], state_conv[l], state_ssm[l])
        ys_l, st_s = _layer(ys, past, p, dec_seq)
        states_p.append(st_p)
        states_s.append(st_s)
        if l + 1 < depth:
            ys = jnp.pad(ys_l, ((0, 0), (0, dec_pad - dec_seq), (0, 0)))
        else:
            ys = ys_l
    stack = lambda states, i: jnp.stack([s[i] for s in states], axis=0)
    return (yp, ys,
            stack(states_p, 0), stack(states_p, 1), stack(states_p, 2), stack(states_p, 3),
            stack(states_p, 4),
            stack(states_s, 0), stack(states_s, 1), stack(states_s, 2), stack(states_s, 3),
            stack(states_s, 4))
```

```python
import functools

import jax
import jax.numpy as jnp
from jax import lax
from jax.experimental import pallas as pl
from jax.experimental.pallas import tpu as pltpu

F32 = jnp.float32
BF16 = jnp.bfloat16

D_MODEL = 1024
N_HEADS_ATT = 16
HEAD_DIM = 64
ATT_WIDTH = N_HEADS_ATT * HEAD_DIM
ATT_SCALE = HEAD_DIM ** -0.5
D_INNER = 2048
SSM_HEAD_DIM = 64
N_HEADS_SSM = D_INNER // SSM_HEAD_DIM
N_GROUPS = 8
HEADS_PER_GROUP = N_HEADS_SSM // N_GROUPS
GROUP_WIDTH = D_INNER // N_GROUPS
D_STATE = 128
CONV_W = 4
CONV_DIM = D_INNER + 2 * N_GROUPS * D_STATE
D_FF = 4 * D_MODEL
EPS = 1e-6
SSD_CHUNK = 64

LANES = 128
SUBLANES = 8
NEG_BIG = -1e30
LOG2E = 1.4426950408889634
VMEM_LIMIT = 56 * 1024 * 1024

SMALL_F0 = 0
SMALL_DT0 = N_HEADS_ATT


def _split3(x):
    hi = x.astype(BF16)
    r1 = x - hi.astype(F32)
    mid = r1.astype(BF16)
    lo = (r1 - mid.astype(F32)).astype(BF16)
    return hi, mid, lo


def _dot(a, b):
    return jnp.dot(a, b, preferred_element_type=F32)


def _dot3(lhs_bf16, x):
    hi, mid, lo = _split3(x)
    return _dot(lhs_bf16, hi) + _dot(lhs_bf16, mid) + _dot(lhs_bf16, lo)


def _dot3_rhs(x, rhs_bf16):
    hi, mid, lo = _split3(x)
    return _dot(hi, rhs_bf16) + _dot(mid, rhs_bf16) + _dot(lo, rhs_bf16)


def _sigmoid(x):
    return 0.5 * jnp.tanh(0.5 * x) + 0.5


def _silu(x):
    h = 0.5 * x
    return h + h * jnp.tanh(h)


def _softplus(x):
    return jnp.maximum(x, 0.0) + jnp.log1p(jnp.exp(-jnp.abs(x)))


def _log_sigmoid(x):
    return jnp.minimum(x, 0.0) - jnp.log1p(jnp.exp(-jnp.abs(x)))


def _lane_iota(shape):
    return lax.broadcasted_iota(jnp.int32, shape, len(shape) - 1)


def _row_iota(shape):
    return lax.broadcasted_iota(jnp.int32, shape, len(shape) - 2)


IN_TN = 512
IN_QK_TILES = 2 * ATT_WIDTH // IN_TN
IN_V_TILES = ATT_WIDTH // IN_TN
IN_ZXG_TILE0 = IN_QK_TILES + IN_V_TILES
IN_ZXG_TILES = (D_INNER + CONV_DIM + 2 * D_MODEL) // IN_TN
IN_MAIN_TILES = IN_ZXG_TILE0 + IN_ZXG_TILES


def _inproj_kernel(x_ref, nw_ref, wm_ref, ws_ref, qk_ref, v_ref, zxg_ref, small_ref, xn_ref):
    j = pl.program_id(1)

    @pl.when(j == 0)
    def _():
        x = x_ref[...]
        ms = jnp.mean(x * x, axis=-1, keepdims=True)
        xn = (x * lax.rsqrt(ms + EPS)) * nw_ref[...]
        xn_ref[...] = xn.astype(BF16)
        small_ref[...] = _dot(xn_ref[...], ws_ref[...])

    @pl.when(j < IN_QK_TILES)
    def _():
        qk_ref[...] = _dot(xn_ref[...], wm_ref[...])

    @pl.when((j >= IN_QK_TILES) & (j < IN_ZXG_TILE0))
    def _():
        v_ref[...] = _dot(xn_ref[...], wm_ref[...])

    @pl.when(j >= IN_ZXG_TILE0)
    def _():
        zxg_ref[...] = _dot(xn_ref[...], wm_ref[...])


def _inproj(x2d, nw, w_main, w_small, tm):
    rows = x2d.shape[0]
    grid = (rows // tm, IN_MAIN_TILES)
    n_zxg = IN_ZXG_TILES
    return pl.pallas_call(
        _inproj_kernel,
        grid=grid,
        in_specs=[
            pl.BlockSpec((tm, D_MODEL), lambda i, j: (i, 0)),
            pl.BlockSpec((1, D_MODEL), lambda i, j: (0, 0)),
            pl.BlockSpec((D_MODEL, IN_TN), lambda i, j: (0, j)),
            pl.BlockSpec((D_MODEL, LANES), lambda i, j: (0, 0)),
        ],
        out_specs=[
            pl.BlockSpec((tm, IN_TN), lambda i, j: (i, jnp.minimum(j, IN_QK_TILES - 1))),
            pl.BlockSpec((tm, IN_TN), lambda i, j: (i, jnp.clip(j - IN_QK_TILES, 0, IN_V_TILES - 1))),
            pl.BlockSpec((tm, IN_TN), lambda i, j: (i, jnp.clip(j - IN_ZXG_TILE0, 0, n_zxg - 1))),
            pl.BlockSpec((tm, LANES), lambda i, j: (i, 0)),
        ],
        out_shape=[
            jax.ShapeDtypeStruct((rows, IN_QK_TILES * IN_TN), F32),
            jax.ShapeDtypeStruct((rows, IN_V_TILES * IN_TN), F32),
            jax.ShapeDtypeStruct((rows, n_zxg * IN_TN), F32),
            jax.ShapeDtypeStruct((rows, LANES), F32),
        ],
        scratch_shapes=[pltpu.VMEM((tm, D_MODEL), BF16)],
        compiler_params=pltpu.CompilerParams(
            dimension_semantics=("arbitrary", "arbitrary"), vmem_limit_bytes=VMEM_LIMIT),
        name="inproj",
    )(x2d, nw, w_main, w_small)


AUG0 = HEAD_DIM
AUG_SHIFT = AUG0 + 6
AUG_W = LANES // N_HEADS_ATT
CP_ONE = 3 * N_HEADS_ATT
CP_SHIFT = CP_ONE + 1
SHIFT_MAX = 45.0
SHIFT_MARGIN = 1.03
VT_ROWS = HEAD_DIM + 16


def _cum_logf(logf128, carry_ref, ltri_ref):
    tp = logf128.shape[0]
    c = carry_ref[0:1, :] + _dot3(ltri_ref[...], logf128)
    carry_ref[...] = jnp.broadcast_to(c[tp - 1:tp, :], carry_ref.shape)
    return c


def _c_parts(c, shift=None):
    hi, mid, lo = _split3(c)
    lane = _lane_iota(c.shape)
    h = N_HEADS_ATT
    tail = jnp.where(lane == CP_ONE, 1.0, 0.0)
    if shift is not None:
        tail = jnp.where(lane == CP_SHIFT, shift, tail)
    packed = jnp.where(
        lane < h, hi.astype(F32),
        jnp.where(lane < 2 * h, pltpu.roll(mid.astype(F32), h, 1),
                  jnp.where(lane < 3 * h, pltpu.roll(lo.astype(F32), 2 * h, 1), tail)))
    return packed.astype(BF16)


def _head_rows(x, extras, out_ref, scale, zero_tail):
    lane = _lane_iota((x.shape[0], LANES))
    for hp in range(N_HEADS_ATT // 2):
        pair = x[:, LANES * hp:LANES * (hp + 1)] * scale
        for o in range(2):
            h = 2 * hp + o
            xh = pair if o == 0 else pltpu.roll(pair, HEAD_DIM, 1)
            ex = pltpu.roll(extras, (AUG0 - AUG_W * h) % LANES, 1)
            if zero_tail:
                ex = jnp.where(lane < AUG0 + AUG_W, ex, 0.0)
            out_ref[0, h] = jnp.where(lane < HEAD_DIM, xh, ex).astype(BF16)


def _write_vt(v, vt_ref):
    vt = v.T
    tail_shape = (VT_ROWS - HEAD_DIM, v.shape[0])
    tail = jnp.where(_row_iota(tail_shape) == 0, 1.0, 0.0).astype(BF16)
    for h in range(N_HEADS_ATT):
        vt_ref[0, h, 0:HEAD_DIM, :] = vt[HEAD_DIM * h:HEAD_DIM * (h + 1), :].astype(BF16)
        vt_ref[0, h, HEAD_DIM:VT_ROWS, :] = tail


def _head_rmsnorm(x, g_ref, gt_ref, w):
    sq = x * x
    hi = sq.astype(BF16)
    lo = (sq - hi.astype(F32)).astype(BF16)
    ssq = _dot(hi, g_ref[...]) + _dot(lo, g_ref[...])
    rs = lax.rsqrt(ssq * (1.0 / HEAD_DIM) + EPS)
    rs_hi = rs.astype(BF16)
    rs_lo = (rs - rs_hi.astype(F32)).astype(BF16)
    rs_e = _dot(rs_hi, gt_ref[...]) + _dot(rs_lo, gt_ref[...])
    return (x * rs_e) * w


def _prep_new_kernel(q_ref, k_ref, v_ref, small_ref, c0_ref, shift_ref, bf_ref, qw_ref, kw_ref,
                     g_ref, gt_ref, selq_ref, selk_ref, ltri_ref, *rest, aliased):
    if aliased:
        rest = rest[2:]
    kout_ref, logf_ref, qaug_ref, kaug_ref, vt_ref, cend_ref, carry_ref = rest
    t = pl.program_id(1)

    @pl.when(t == 0)
    def _():
        carry_ref[...] = c0_ref[0]

    logf = _log_sigmoid(small_ref[...] + bf_ref[...])
    logf_ref[...] = logf[:, SMALL_F0:SMALL_F0 + N_HEADS_ATT]
    c = _cum_logf(logf, carry_ref, ltri_ref)
    cend_ref[0] = carry_ref[...]
    cp = _c_parts(c * LOG2E, shift_ref[0, 0:1, :])

    qn = _head_rmsnorm(q_ref[...], g_ref, gt_ref, qw_ref[...])
    _head_rows(qn, _dot(cp, selq_ref[...]), qaug_ref, ATT_SCALE * LOG2E, False)
    kn = _head_rmsnorm(k_ref[...], g_ref, gt_ref, kw_ref[...])
    _head_rows(kn, _dot(cp, selk_ref[...]), kaug_ref, 1.0, True)
    kout_ref[...] = kn
    _write_vt(v_ref[...], vt_ref)


def _prep_cache_kernel(k_ref, v_ref, logf_ref, c0_ref, g_ref, selk_ref, ltri_ref,
                       kaug_ref, vt_ref, cend_ref, kmax_ref, carry_ref):
    t = pl.program_id(1)

    @pl.when(t == 0)
    def _():
        carry_ref[...] = c0_ref[0]
        kmax_ref[...] = jnp.zeros(kmax_ref.shape, F32)

    c = _cum_logf(logf_ref[0], carry_ref, ltri_ref)
    cend_ref[0] = carry_ref[...]
    k = k_ref[0]
    _head_rows(k, _dot(_c_parts(c * LOG2E), selk_ref[...]), kaug_ref, 1.0, True)
    _write_vt(v_ref[0], vt_ref)
    sq = k * k
    hi = sq.astype(BF16)
    lo = (sq - hi.astype(F32)).astype(BF16)
    ssq = _dot(hi, g_ref[...]) + _dot(lo, g_ref[...])
    kmax_ref[0] = jnp.maximum(kmax_ref[0], jnp.max(ssq, axis=0, keepdims=True))


def _aug_selectors():
    h = N_HEADS_ATT
    rows = jnp.arange(LANES)[:, None]
    cols = jnp.arange(LANES)[None, :]
    head = cols // AUG_W
    pos = cols % AUG_W
    one_row = rows == CP_ONE
    selq = jnp.zeros((LANES, LANES), F32)
    selk = jnp.zeros((LANES, LANES), F32)
    for part in range(3):
        part_row = rows == part * h + head
        selq = selq + jnp.where(part_row & (pos == part), 1.0, 0.0)
        selq = selq + jnp.where(one_row & (pos == 3 + part), 1.0, 0.0)
        selk = selk + jnp.where(one_row & (pos == part), 1.0, 0.0)
        selk = selk - jnp.where(part_row & (pos == 3 + part), 1.0, 0.0)
    selq = selq - jnp.where((rows == CP_SHIFT) & (pos == AUG_SHIFT - AUG0), 1.0, 0.0)
    selk = selk + jnp.where(one_row & (pos == AUG_SHIFT - AUG0), 1.0, 0.0)
    return selq.astype(BF16), selk.astype(BF16)


def _head_group_matrices():
    j = jnp.arange(ATT_WIDTH)[:, None] // HEAD_DIM
    h = jnp.arange(LANES)[None, :]
    g = jnp.where(j == h, 1.0, 0.0).astype(BF16)
    return g, g.T


def _ltri(n, block=None):
    r = jnp.arange(n)[:, None]
    c = jnp.arange(n)[None, :]
    keep = c <= r
    if block is not None:
        keep = keep & (r // block == c // block)
    return jnp.where(keep, 1.0, 0.0).astype(BF16)


def _prep_new(qk, v, small, c0, shift, bf128, qw, kw, batch, seq, tk, bufs=None, t_off_blocks=0):
    nt = seq // tk
    h = N_HEADS_ATT
    g, gt = _head_group_matrices()
    selq, selk = _aug_selectors()
    ltri = _ltri(tk)
    rows = batch * seq
    const = lambda shape: pl.BlockSpec(shape, lambda b, t: tuple(0 for _ in shape))
    in_specs = [
        pl.BlockSpec((tk, ATT_WIDTH), lambda b, t: (b * nt + t, 0)),
        pl.BlockSpec((tk, ATT_WIDTH), lambda b, t: (b * nt + t, 1)),
        pl.BlockSpec((tk, ATT_WIDTH), lambda b, t: (b * nt + t, 0)),
        pl.BlockSpec((tk, LANES), lambda b, t: (b * nt + t, 0)),
        pl.BlockSpec((1, SUBLANES, LANES), lambda b, t: (b, 0, 0)),
        pl.BlockSpec((1, SUBLANES, LANES), lambda b, t: (b, 0, 0)),
        const((1, LANES)), const((1, ATT_WIDTH)), const((1, ATT_WIDTH)),
        const((ATT_WIDTH, LANES)), const((LANES, ATT_WIDTH)),
        const((LANES, LANES)), const((LANES, LANES)), const((tk, tk)),
    ]
    args = [qk, qk, v, small, c0, shift, bf128, qw, kw, g, gt, selq, selk, ltri]
    if bufs is None:
        kv_len = seq
        aliases = {}
    else:
        kv_len = bufs[0].shape[2]
        in_specs += [pl.BlockSpec(memory_space=pl.ANY), pl.BlockSpec(memory_space=pl.ANY)]
        aliases = {len(args): 3, len(args) + 1: 4}
        args += list(bufs)
    o = t_off_blocks
    out_specs = [
        pl.BlockSpec((tk, ATT_WIDTH), lambda b, t: (b * nt + t, 0)),
        pl.BlockSpec((tk, h), lambda b, t: (b * nt + t, 0)),
        pl.BlockSpec((1, h, tk, LANES), lambda b, t: (b, 0, t, 0)),
        pl.BlockSpec((1, h, tk, LANES), lambda b, t: (b, 0, t + o, 0)),
        pl.BlockSpec((1, h, VT_ROWS, tk), lambda b, t: (b, 0, 0, t + o)),
        pl.BlockSpec((1, SUBLANES, LANES), lambda b, t: (b, 0, 0)),
    ]
    out_shape = [
        jax.ShapeDtypeStruct((rows, ATT_WIDTH), F32),
        jax.ShapeDtypeStruct((rows, h), F32),
        jax.ShapeDtypeStruct((batch, h, seq, LANES), BF16),
        jax.ShapeDtypeStruct((batch, h, kv_len, LANES), BF16),
        jax.ShapeDtypeStruct((batch, h, VT_ROWS, kv_len), BF16),
        jax.ShapeDtypeStruct((batch, SUBLANES, LANES), F32),
    ]
    return pl.pallas_call(
        functools.partial(_prep_new_kernel, aliased=bufs is not None),
        grid=(batch, nt),
        in_specs=in_specs,
        out_specs=out_specs,
        out_shape=out_shape,
        scratch_shapes=[pltpu.VMEM((SUBLANES, LANES), F32)],
        input_output_aliases=aliases,
        compiler_params=pltpu.CompilerParams(
            dimension_semantics=("arbitrary", "arbitrary"), vmem_limit_bytes=VMEM_LIMIT),
        name="prep_new",
    )(*args)


def _prep_cache(cache_k, cache_v, cache_logf128, c0, tk, kv_len):
    batch, past, _ = cache_k.shape
    nt = past // tk
    h = N_HEADS_ATT
    g, _ = _head_group_matrices()
    _, selk = _aug_selectors()
    ltri = _ltri(tk)
    const = lambda shape: pl.BlockSpec(shape, lambda b, t: tuple(0 for _ in shape))
    return pl.pallas_call(
        _prep_cache_kernel,
        grid=(batch, nt),
        in_specs=[
            pl.BlockSpec((1, tk, ATT_WIDTH), lambda b, t: (b, t, 0)),
            pl.BlockSpec((1, tk, ATT_WIDTH), lambda b, t: (b, t, 0)),
            pl.BlockSpec((1, tk, LANES), lambda b, t: (b, t, 0)),
            pl.BlockSpec((1, SUBLANES, LANES), lambda b, t: (b, 0, 0)),
            const((ATT_WIDTH, LANES)), const((LANES, LANES)), const((tk, tk)),
        ],
        out_specs=[
            pl.BlockSpec((1, h, tk, LANES), lambda b, t: (b, 0, t, 0)),
            pl.BlockSpec((1, h, VT_ROWS, tk), lambda b, t: (b, 0, 0, t)),
            pl.BlockSpec((1, SUBLANES, LANES), lambda b, t: (b, 0, 0)),
            pl.BlockSpec((1, 1, LANES), lambda b, t: (b, 0, 0)),
        ],
        out_shape=[
            jax.ShapeDtypeStruct((batch, h, kv_len, LANES), BF16),
            jax.ShapeDtypeStruct((batch, h, VT_ROWS, kv_len), BF16),
            jax.ShapeDtypeStruct((batch, SUBLANES, LANES), F32),
            jax.ShapeDtypeStruct((batch, 1, LANES), F32),
        ],
        scratch_shapes=[pltpu.VMEM((SUBLANES, LANES), F32)],
        compiler_params=pltpu.CompilerParams(
            dimension_semantics=("arbitrary", "arbitrary"), vmem_limit_bytes=VMEM_LIMIT),
        name="prep_cache",
    )(cache_k, cache_v, cache_logf128, c0, g, selk, ltri)


def _flash_kernel(q_ref, k_ref, vt_ref, o_ref, s0_ref, s1_ref, cm_ref, m_ref, acc_ref,
                  *, tq, tkf, q_off):
    qi = pl.program_id(2)
    q_start = q_off + qi * tq
    n_full = q_start // tkf
    acc_ref[...] = jnp.zeros(acc_ref.shape, F32)
    m_ref[...] = jnp.full(m_ref.shape, NEG_BIG, F32)
    bufs = (s0_ref, s1_ref)

    def scores(k_start, rows, masked, slot):
        for o in range(2):
            kc = k_ref[0, o, pl.ds(k_start, rows), :]
            st = lax.dot_general(kc, q_ref[0, o], (((1,), (1,)), ((), ())),
                                 preferred_element_type=F32)
            if masked:
                st = jnp.where(_row_iota(st.shape) <= _lane_iota(st.shape), st, NEG_BIG)
            bufs[slot][o, 0:rows, :] = st
            cm_ref[slot, o] = jnp.max(st, axis=0, keepdims=True)

    def consume(k_start, rows, slot):
        for o in range(2):
            m_prev = m_ref[o]
            m_new = jnp.maximum(m_prev, cm_ref[slot, o])
            alpha = jnp.exp2(m_prev - m_new)
            p = jnp.exp2(bufs[slot][o, 0:rows, :] - m_new)
            pv = _dot(vt_ref[0, o, :, pl.ds(k_start, rows)], p.astype(BF16))
            acc_ref[o] = alpha * acc_ref[o] + pv
            m_ref[o] = m_new

    full_start = lambda f: pl.multiple_of(f * tkf, tkf)
    diag_start = pl.multiple_of(q_start, tq)
    scores(diag_start, tq, True, 0)

    @pl.when(n_full > 0)
    def _():
        scores(full_start(0), tkf, False, 1)

    consume(diag_start, tq, 0)

    pairs = jnp.maximum(n_full - 1, 0) // 2

    def body(t, carry):
        f = 2 * t
        scores(full_start(f + 1), tkf, False, 0)
        consume(full_start(f), tkf, 1)
        scores(full_start(f + 2), tkf, False, 1)
        consume(full_start(f + 1), tkf, 0)
        return carry

    lax.fori_loop(0, pairs, body, 0)
    f_rem = 2 * pairs
    left = n_full - f_rem

    @pl.when(left == 2)
    def _():
        scores(full_start(f_rem + 1), tkf, False, 0)
        consume(full_start(f_rem), tkf, 1)
        consume(full_start(f_rem + 1), tkf, 0)

    @pl.when(left == 1)
    def _():
        consume(full_start(f_rem), tkf, 1)

    outs = [acc_ref[o, 0:HEAD_DIM, :] / acc_ref[o, HEAD_DIM:HEAD_DIM + 1, :] for o in range(2)]
    o_ref[0] = jnp.concatenate(outs, axis=0).T.astype(BF16)


FLASH_TRIP_KEYS = 2048


def _flash_fixed_kernel(q_ref, k_ref, vt_ref, o_ref, pa_ref, pb_ref, acc_ref, *, tq, tkf, q_off):
    qi = pl.program_id(2)
    q_start = q_off + qi * tq
    n_full = q_start // tkf
    slot_of = lambda f_parity: pa_ref if f_parity else pb_ref

    def probs(k_start, rows, masked, p_ref):
        for o in range(2):
            kc = k_ref[0, o, pl.ds(k_start, rows), :]
            st = lax.dot_general(kc, q_ref[0, o], (((1,), (1,)), ((), ())),
                                 preferred_element_type=F32)
            if masked:
                st = jnp.where(_row_iota(st.shape) <= _lane_iota(st.shape), st, NEG_BIG)
            p_ref[o, 0:rows, :] = jnp.exp2(st).astype(BF16)

    def accumulate(k_start, rows, p_ref, first=False):
        for o in range(2):
            pv = _dot(vt_ref[0, o, :, pl.ds(k_start, rows)], p_ref[o, 0:rows, :])
            acc_ref[o] = pv if first else acc_ref[o] + pv

    full_start = lambda f: pl.multiple_of(f * tkf, tkf)
    diag_start = pl.multiple_of(q_start, tq)
    probs(diag_start, tq, True, pa_ref)

    if tq == tkf:
        start_of = lambda g: pl.multiple_of(jnp.where(g == 0, q_start, (g - 1) * tkf), tkf)
        acc_ref[...] = jnp.zeros(acc_ref.shape, F32)
        n_steps = n_full
        enabled = True

        def step(g, parity):
            probs(full_start(g), tkf, False, slot_of(parity))
            accumulate(start_of(g), tkf, slot_of(1 - parity))

        last = lambda g, parity: accumulate(start_of(g), tkf, slot_of(1 - parity))
    else:
        @pl.when(n_full > 0)
        def _():
            probs(full_start(0), tkf, False, slot_of(0))

        accumulate(diag_start, tq, pa_ref, first=True)
        n_steps = jnp.maximum(n_full - 1, 0)
        enabled = n_full > 0

        def step(f, parity):
            probs(full_start(f + 1), tkf, False, slot_of(1 - parity))
            accumulate(full_start(f), tkf, slot_of(parity))

        last = lambda f, parity: accumulate(full_start(f), tkf, slot_of(parity))

    unroll = max(2, FLASH_TRIP_KEYS // tkf)

    def body(t, carry):
        for i in range(unroll):
            step(unroll * t + i, i % 2)
        return carry

    lax.fori_loop(0, n_steps // unroll, body, 0)
    f0 = (n_steps // unroll) * unroll
    left = n_steps - f0
    for r in range(unroll):
        @pl.when(enabled & (left == r))
        def _(r=r):
            for i in range(r):
                step(f0 + i, i % 2)
            last(f0 + r, r % 2)

    outs = [acc_ref[o, 0:HEAD_DIM, :] / acc_ref[o, HEAD_DIM:HEAD_DIM + 1, :] for o in range(2)]
    o_ref[0] = jnp.concatenate(outs, axis=0).T.astype(BF16)


def _flash(q_aug, k_aug, vt, tq, tkf, q_off, fixed_reference):
    batch, h, tq_total, _ = q_aug.shape
    kv_len = k_aug.shape[2]
    nq = tq_total // tq
    if fixed_reference:
        body = _flash_fixed_kernel
        scratch = [
            pltpu.VMEM((2, max(tq, tkf), tq), BF16),
            pltpu.VMEM((2, max(tq, tkf), tq), BF16),
            pltpu.VMEM((2, VT_ROWS, tq), F32),
        ]
    else:
        body = _flash_kernel
        scratch = [
            pltpu.VMEM((2, max(tq, tkf), tq), F32),
            pltpu.VMEM((2, max(tq, tkf), tq), F32),
            pltpu.VMEM((2, 2, 1, tq), F32),
            pltpu.VMEM((2, 1, tq), F32),
            pltpu.VMEM((2, VT_ROWS, tq), F32),
        ]
    return pl.pallas_call(
        functools.partial(body, tq=tq, tkf=tkf, q_off=q_off),
        grid=(batch, h // 2, nq),
        in_specs=[
            pl.BlockSpec((1, 2, tq, LANES), lambda b, hp, qi: (b, hp, qi, 0)),
            pl.BlockSpec((1, 2, kv_len, LANES), lambda b, hp, qi: (b, hp, 0, 0)),
            pl.BlockSpec((1, 2, VT_ROWS, kv_len), lambda b, hp, qi: (b, hp, 0, 0)),
        ],
        out_specs=pl.BlockSpec((1, tq, LANES), lambda b, hp, qi: (b, qi, hp)),
        out_shape=jax.ShapeDtypeStruct((batch, tq_total, ATT_WIDTH), BF16),
        scratch_shapes=scratch,
        compiler_params=pltpu.CompilerParams(
            dimension_semantics=("arbitrary", "arbitrary", "arbitrary"),
            vmem_limit_bytes=VMEM_LIMIT),
        name="flash_fixed" if fixed_reference else "flash",
    )(q_aug, k_aug, vt)


B_LANE0 = D_INNER
C_LANE0 = D_INNER + N_GROUPS * D_STATE
PK_HI0 = SMALL_DT0
PK_MID0 = PK_HI0 + N_HEADS_SSM
PK_LO0 = PK_MID0 + N_HEADS_SSM


def _pack_head_parts(x):
    hi, mid, lo = _split3(x)
    lane = _lane_iota(x.shape)
    packed = jnp.where(
        lane < PK_MID0, hi.astype(F32),
        jnp.where(lane < PK_LO0, pltpu.roll(mid.astype(F32), N_HEADS_SSM, 1),
                  pltpu.roll(lo.astype(F32), 2 * N_HEADS_SSM, 1)))
    return packed.astype(BF16)


def _head_expand_matrix():
    rows = jnp.arange(LANES)[:, None]
    head = jnp.arange(D_INNER)[None, :] // SSM_HEAD_DIM
    hit = (rows == PK_HI0 + head) | (rows == PK_MID0 + head) | (rows == PK_LO0 + head)
    return jnp.where(hit, 1.0, 0.0).astype(BF16)


def _ssd_kernel(z_ref, xs_ref, bc_ref, small_ref, cpast_ref, h0_ref, cw_ref, cb_ref,
                dtb_ref, alog_ref, dskip_ref, nw_ref, lblk_ref, sel_ref,
                y_ref, cnew_ref, hlast_ref,
                xbuf_ref, ht_ref, ae_ref, dte_ref, ybuf_ref, *, tb, n_valid, tail_row):
    t = pl.program_id(1)
    nt = pl.num_programs(1)
    n_state_blocks = D_INNER // LANES

    n_conv_slabs = CONV_DIM // LANES
    n_inner_slabs = D_INNER // LANES

    @pl.when(t == 0)
    def _():
        for j in range(n_conv_slabs):
            xbuf_ref[j, 0:SUBLANES, :] = cpast_ref[0, :, LANES * j:LANES * (j + 1)]
        for i in range(n_state_blocks):
            ht_ref[:, LANES * i:LANES * (i + 1)] = h0_ref[0, LANES * i:LANES * (i + 1), :].T

    for j in range(n_inner_slabs):
        xbuf_ref[j, SUBLANES:SUBLANES + tb, :] = xs_ref[:, LANES * j:LANES * (j + 1)]
        xbuf_ref[n_inner_slabs + j, SUBLANES:SUBLANES + tb, :] = bc_ref[:, LANES * j:LANES * (j + 1)]

    lane = _lane_iota((tb, LANES))
    row = _row_iota((tb, LANES)) + t * tb
    live = (lane >= SMALL_DT0) & (lane < SMALL_DT0 + N_HEADS_SSM) & (row < n_valid)
    dt = jnp.where(live, _softplus(small_ref[...] + dtb_ref[...]), 0.0)
    a = dt * (-jnp.exp(alog_ref[...]))
    acum = _dot3(lblk_ref[...], a)
    ae_all = _dot(_pack_head_parts(acum), sel_ref[...])
    dte_all = _dot(_pack_head_parts(dt), sel_ref[...])
    for j in range(n_inner_slabs):
        ae_ref[j] = ae_all[:, LANES * j:LANES * (j + 1)]
        dte_ref[j] = dte_all[:, LANES * j:LANES * (j + 1)]

    gw = GROUP_WIDTH
    token_of = lambda r: 32 * (r // 32) + (r // SUBLANES) % 4 + 4 * (r % SUBLANES)
    sub = _row_iota((SSD_CHUNK, gw))
    pos = _lane_iota((SSD_CHUNK, gw)) % SSM_HEAD_DIM
    diag = pos == sub
    causal = token_of(pos) <= token_of(sub)
    blk_row = _row_iota((gw, gw)) // SSM_HEAD_DIM
    blk_col = _lane_iota((gw, gw)) // SSM_HEAD_DIM
    same_head = blk_row == blk_col
    windows = [(32 * half + a) for half in range(2) for a in range(4)]

    def load_perm(ref, j, row0):
        return jnp.concatenate(
            [ref[j, pl.ds(row0 + w, SUBLANES, stride=4), :] for w in windows], axis=0)

    def conv_slab(c0, j):
        cols = slice(LANES * j, LANES * (j + 1))
        out = cb_ref[:, cols]
        for i in range(CONV_W):
            out = out + load_perm(xbuf_ref, j, c0 + SUBLANES - (CONV_W - 1) + i) * cw_ref[i:i + 1, cols]
        return _silu(out)

    def chunk(c, carry):
        c0 = pl.multiple_of(c * SSD_CHUNK, SSD_CHUNK)
        rows = pl.ds(c0, SSD_CHUNK)
        for g in range(N_GROUPS):
            lanes = slice(gw * g, gw * (g + 1))
            j0 = 2 * g
            xs = jnp.concatenate([conv_slab(c0, j0), conv_slab(c0, j0 + 1)], axis=1)
            bm = conv_slab(c0, n_inner_slabs + g)
            cm = conv_slab(c0, n_inner_slabs + N_GROUPS + g)
            ae = jnp.concatenate([load_perm(ae_ref, j0, c0), load_perm(ae_ref, j0 + 1, c0)], axis=1)
            dte = jnp.concatenate([load_perm(dte_ref, j0, c0), load_perm(dte_ref, j0 + 1, c0)], axis=1)
            xdt = xs * dte
            bm16 = bm.astype(BF16)
            cm16 = cm.astype(BF16)

            b4 = jnp.concatenate([bm16] * HEADS_PER_GROUP, axis=0)
            cb4 = lax.dot_general(cm16, b4, (((1,), (1,)), ((), ())), preferred_element_type=F32)
            a_row = jnp.sum(jnp.where(diag, ae, 0.0), axis=0, keepdims=True)
            decay = jnp.exp(jnp.where(causal, ae - a_row, NEG_BIG))
            xdt16 = xdt.astype(BF16)
            x4 = jnp.where(same_head, jnp.concatenate([xdt16] * HEADS_PER_GROUP, axis=0),
                           jnp.zeros((), BF16))
            y = _dot((cb4 * decay).astype(BF16), x4)

            hprev = ht_ref[:, lanes]
            y = y + jnp.exp(ae) * _dot(cm16, hprev.astype(BF16))
            a_last = ae[SSD_CHUNK - 1:SSD_CHUNK, :]
            xw = (xdt * jnp.exp(a_last - ae)).astype(BF16)
            upd = lax.dot_general(bm16, xw, (((0,), (0,)), ((), ())), preferred_element_type=F32)
            ht_ref[:, lanes] = hprev * jnp.exp(a_last) + upd

            y = y + xs * dskip_ref[:, lanes]
            for jj in range(2):
                for v, w in enumerate(windows):
                    ybuf_ref[jj, pl.ds(c0 + w, SUBLANES, stride=4), :] = (
                        y[SUBLANES * v:SUBLANES * (v + 1), LANES * jj:LANES * (jj + 1)])
            y = jnp.concatenate([ybuf_ref[0, rows, :], ybuf_ref[1, rows, :]], axis=1)
            y = y * _silu(z_ref[rows, lanes])
            ms = jnp.mean(y * y, axis=-1, keepdims=True)
            y_ref[rows, lanes] = ((y * lax.rsqrt(ms + EPS)) * nw_ref[:, lanes]).astype(BF16)
        return carry

    lax.fori_loop(0, tb // SSD_CHUNK, chunk, 0)

    for j in range(n_conv_slabs):
        cnew_ref[0, :, LANES * j:LANES * (j + 1)] = xbuf_ref[j, tail_row:tail_row + SUBLANES, :]
        xbuf_ref[j, 0:SUBLANES, :] = xbuf_ref[j, tb:tb + SUBLANES, :]

    @pl.when(t == nt - 1)
    def _():
        for i in range(n_state_blocks):
            hlast_ref[0, LANES * i:LANES * (i + 1), :] = ht_ref[:, LANES * i:LANES * (i + 1)].T


def _ssd(zxg, small, conv_past8, h0, cw, cb, dtb128, alog128, dskip_e, nw, batch, seq, tb, n_valid):
    nt = seq // tb
    rows = batch * seq
    tail_row = n_valid - (seq - tb)
    lblk = _ltri(tb, SSD_CHUNK)
    sel = _head_expand_matrix()
    const = lambda shape: pl.BlockSpec(shape, lambda b, t: tuple(0 for _ in shape))
    return pl.pallas_call(
        functools.partial(_ssd_kernel, tb=tb, n_valid=n_valid, tail_row=tail_row),
        grid=(batch, nt),
        in_specs=[
            pl.BlockSpec((tb, D_INNER), lambda b, t: (b * nt + t, 0)),
            pl.BlockSpec((tb, D_INNER), lambda b, t: (b * nt + t, 1)),
            pl.BlockSpec((tb, D_INNER), lambda b, t: (b * nt + t, 2)),
            pl.BlockSpec((tb, LANES), lambda b, t: (b * nt + t, 0)),
            pl.BlockSpec((1, SUBLANES, CONV_DIM), lambda b, t: (b, 0, 0)),
            pl.BlockSpec((1, D_INNER, D_STATE), lambda b, t: (b, 0, 0)),
            const((CONV_W, CONV_DIM)), const((1, CONV_DIM)),
            const((1, LANES)), const((1, LANES)), const((1, D_INNER)), const((1, D_INNER)),
            const((tb, tb)), const((LANES, D_INNER)),
        ],
        out_specs=[
            pl.BlockSpec((tb, D_INNER), lambda b, t: (b * nt + t, 0)),
            pl.BlockSpec((1, SUBLANES, CONV_DIM), lambda b, t: (b, 0, 0)),
            pl.BlockSpec((1, D_INNER, D_STATE), lambda b, t: (b, 0, 0)),
        ],
        out_shape=[
            jax.ShapeDtypeStruct((rows, D_INNER), BF16),
            jax.ShapeDtypeStruct((batch, SUBLANES, CONV_DIM), F32),
            jax.ShapeDtypeStruct((batch, D_INNER, D_STATE), F32),
        ],
        scratch_shapes=[
            pltpu.VMEM((CONV_DIM // LANES, tb + SUBLANES, LANES), F32),
            pltpu.VMEM((D_STATE, D_INNER), F32),
            pltpu.VMEM((D_INNER // LANES, tb, LANES), F32),
            pltpu.VMEM((D_INNER // LANES, tb, LANES), F32),
            pltpu.VMEM((2, tb, LANES), F32),
        ],
        compiler_params=pltpu.CompilerParams(
            dimension_semantics=("arbitrary", "arbitrary"), vmem_limit_bytes=VMEM_LIMIT),
        name="ssd",
    )(zxg, zxg, zxg, small, conv_past8, h0, cw, cb, dtb128, alog128, dskip_e, nw, lblk, sel)


FF_CHUNK = 1024


def _merge_kernel(att_ref, ys_ref, ga_ref, gs_ref, x_ref, wpa_ref, wpb_ref, wout_ref,
                  n2_ref, wup_ref, wdn_ref, o_ref):
    merged = (_sigmoid(ga_ref[...]) * _dot(att_ref[...], wpa_ref[...])
              + _sigmoid(gs_ref[...]) * _dot(ys_ref[...], wpb_ref[...]))
    x1 = x_ref[...] + _dot(merged.astype(BF16), wout_ref[...])
    ms = jnp.mean(x1 * x1, axis=-1, keepdims=True)
    hn = ((x1 * lax.rsqrt(ms + EPS)) * n2_ref[...]).astype(BF16)
    acc = x1
    for c in range(D_FF // FF_CHUNK):
        cols = slice(FF_CHUNK * c, FF_CHUNK * (c + 1))
        up = jnp.maximum(_dot(hn, wup_ref[:, cols]), 0.0)
        acc = acc + _dot((up * up).astype(BF16), wdn_ref[cols, :])
    o_ref[...] = acc


def _merge(att, ys, zxg, x2d, wpa, wpb, wout, n2, wup, wdn, tm):
    rows = x2d.shape[0]
    ga_blk = 6
    gs_blk = 7
    resident = lambda shape: pl.BlockSpec(shape, lambda i: tuple(0 for _ in shape),
                                          pipeline_mode=pl.Buffered(1))
    return pl.pallas_call(
        _merge_kernel,
        grid=(rows // tm,),
        in_specs=[
            pl.BlockSpec((tm, ATT_WIDTH), lambda i: (i, 0)),
            pl.BlockSpec((tm, D_INNER), lambda i: (i, 0)),
            pl.BlockSpec((tm, D_MODEL), lambda i: (i, ga_blk)),
            pl.BlockSpec((tm, D_MODEL), lambda i: (i, gs_blk)),
            pl.BlockSpec((tm, D_MODEL), lambda i: (i, 0)),
            resident((ATT_WIDTH, D_MODEL)), resident((D_INNER, D_MODEL)),
            resident((D_MODEL, D_MODEL)), resident((1, D_MODEL)),
            resident((D_MODEL, D_FF)), resident((D_FF, D_MODEL)),
        ],
        out_specs=pl.BlockSpec((tm, D_MODEL), lambda i: (i, 0)),
        out_shape=jax.ShapeDtypeStruct((rows, D_MODEL), F32),
        compiler_params=pltpu.CompilerParams(
            dimension_semantics=("arbitrary",), vmem_limit_bytes=VMEM_LIMIT),
        name="merge",
    )(att, ys, zxg, zxg, x2d, wpa, wpb, wout, n2, wup, wdn)


def _pick_tile(n, prefs):
    for p in prefs:
        if n % p == 0:
            return p
    raise ValueError(f"no tile in {prefs} divides {n}")


def _pad_lanes(vec, lane0):
    return jnp.zeros((1, LANES), F32).at[0, lane0:lane0 + vec.shape[0]].set(vec.astype(F32))


def _layer(x, past, p, n_valid):
    batch, seq, _ = x.shape
    rows = batch * seq
    x2d = x.reshape(rows, D_MODEL)

    tm = _pick_tile(rows, (2048, 1024, 512, 256, 128))
    qk, v, zxg, small = _inproj(x2d, p["norm1_w"], p["w_main"], p["w_small"], tm)

    tk = _pick_tile(seq, (512, 256, 128)) if past is None else LANES
    zeros_c = jnp.zeros((batch, SUBLANES, LANES), F32)
    k2_new = HEAD_DIM * jnp.max(jnp.square(p["kw"]))
    if past is None:
        k2 = jnp.full((batch,), k2_new, F32)
    else:
        cache_k, cache_v, cache_logf, conv_past, h0 = past
        n_past = cache_k.shape[1]
        kv_len = n_past + seq
        logf128 = jnp.pad(cache_logf.astype(F32), ((0, 0), (0, 0), (0, LANES - N_HEADS_ATT)))
        tkf = _pick_tile(n_past, (1024, 512, 256, 128))
        k_buf, vt_buf, c_past, k2_cache = _prep_cache(cache_k, cache_v, logf128, zeros_c, tkf, kv_len)
        k2 = jnp.maximum(k2_new, jnp.max(k2_cache, axis=(1, 2)))
    q_norm = jnp.sqrt(HEAD_DIM * jnp.max(jnp.square(p["qw"])))
    shift = (SHIFT_MARGIN * ATT_SCALE * LOG2E) * q_norm * jnp.sqrt(k2)
    shift = shift.astype(BF16).astype(F32)
    shift_rows = jnp.broadcast_to(shift[:, None, None], (batch, SUBLANES, LANES))
    if past is None:
        k_out, logf, q_aug, k_aug, vt, _ = _prep_new(
            qk, v, small, zeros_c, shift_rows, p["bf128"], p["qw"], p["kw"], batch, seq, tk)
        tkf, q_off = tk, 0
        conv_past8 = jnp.zeros((batch, SUBLANES, CONV_DIM), F32)
        h0 = jnp.zeros((batch, D_INNER, D_STATE), F32)
    else:
        k_out, logf, q_aug, k_aug, vt, _ = _prep_new(
            qk, v, small, c_past, shift_rows, p["bf128"], p["qw"], p["kw"], batch, seq, tk,
            bufs=(k_buf, vt_buf), t_off_blocks=n_past // tk)
        q_off = n_past
        conv_past8 = jnp.pad(conv_past.astype(F32), ((0, 0), (SUBLANES - (CONV_W - 1), 0), (0, 0)))
        h0 = h0.astype(F32).reshape(batch, D_INNER, D_STATE)

    att = lax.cond(jnp.max(shift) <= SHIFT_MAX,
                   lambda: _flash(q_aug, k_aug, vt, tk, tkf, q_off, True),
                   lambda: _flash(q_aug, k_aug, vt, tk, tkf, q_off, False))

    tb = _pick_tile(seq, (256, 128))
    ys, conv_new8, h_last = _ssd(zxg, small, conv_past8, h0, p["conv_w"], p["conv_b"],
                                 p["dtb128"], p["alog128"], p["dskip_e"], p["ssm_norm_w"],
                                 batch, seq, tb, n_valid)

    tmm = _pick_tile(rows, (512, 256, 128))
    y = _merge(att.reshape(rows, ATT_WIDTH), ys, zxg, x2d, p["w_pa"], p["w_pb"], p["w_out"],
               p["norm2_w"], p["w_up"], p["w_down"], tmm)

    nv = n_valid
    y = y.reshape(batch, seq, D_MODEL)[:, :nv]
    k_new = k_out.reshape(batch, seq, N_HEADS_ATT, HEAD_DIM)[:, :nv]
    v_new = v.reshape(batch, seq, N_HEADS_ATT, HEAD_DIM)[:, :nv]
    logf_new = logf.reshape(batch, seq, N_HEADS_ATT)[:, :nv]
    conv_new = conv_new8[:, SUBLANES - (CONV_W - 1):, :]
    h_new = h_last.reshape(batch, N_HEADS_SSM, SSM_HEAD_DIM, D_STATE)
    return y, (k_new, v_new, logf_new, conv_new, h_new)


def _layer_params(l, norm1_w, w_in, b_f, q_norm_w, k_norm_w, conv_w, conv_b, dt_bias, A_log,
                  D_skip, ssm_norm_w, w_pa, w_pb, w_out, norm2_w, w_up, w_down):
    w = w_in[l]
    o = 0
    seg = {}
    for name, size in (("q", ATT_WIDTH), ("k", ATT_WIDTH), ("v", ATT_WIDTH), ("f", N_HEADS_ATT),
                       ("z", D_INNER), ("xbc", CONV_DIM), ("dt", N_HEADS_SSM),
                       ("ga", D_MODEL), ("gs", D_MODEL)):
        seg[name] = w[:, o:o + size]
        o += size
    w_main = jnp.concatenate([seg[n] for n in ("q", "k", "v", "z", "xbc", "ga", "gs")], axis=1)
    w_small = jnp.zeros((D_MODEL, LANES), w.dtype)
    w_small = w_small.at[:, SMALL_F0:SMALL_F0 + N_HEADS_ATT].set(seg["f"])
    w_small = w_small.at[:, SMALL_DT0:SMALL_DT0 + N_HEADS_SSM].set(seg["dt"])
    row = lambda a: a.astype(F32).reshape(1, -1)
    return {
        "norm1_w": row(norm1_w[l]),
        "w_main": w_main.astype(BF16),
        "w_small": w_small.astype(BF16),
        "bf128": _pad_lanes(b_f[l], SMALL_F0),
        "qw": row(jnp.tile(q_norm_w[l], N_HEADS_ATT)),
        "kw": row(jnp.tile(k_norm_w[l], N_HEADS_ATT)),
        "conv_w": conv_w[l].astype(F32),
        "conv_b": row(conv_b[l]),
        "dtb128": _pad_lanes(dt_bias[l], SMALL_DT0),
        "alog128": _pad_lanes(A_log[l], SMALL_DT0),
        "dskip_e": row(jnp.repeat(D_skip[l], SSM_HEAD_DIM)),
        "ssm_norm_w": row(ssm_norm_w[l]),
        "w_pa": w_pa[l].astype(BF16),
        "w_pb": w_pb[l].astype(BF16),
        "w_out": w_out[l].astype(BF16),
        "norm2_w": row(norm2_w[l]),
        "w_up": w_up[l].astype(BF16),
        "w_down": w_down[l].astype(BF16),
    }


def kernel(x_prompt, x_sample, cache_k, cache_v, cache_logf, state_conv, state_ssm, norm1_w, w_in,
           b_f, q_norm_w, k_norm_w, conv_w, conv_b, dt_bias, A_log, D_skip, ssm_norm_w, w_pa, w_pb,
           w_out, norm2_w, w_up, w_down):
    depth = w_in.shape[0]
    dec_batch, dec_seq, _ = x_sample.shape
    dec_pad = -(-dec_seq // LANES) * LANES
    yp = x_prompt
    ys = jnp.pad(x_sample, ((0, 0), (0, dec_pad - dec_seq), (0, 0)))
    states_p, states_s = [], []
    for l in range(depth):
        p = _layer_params(l, norm1_w, w_in, b_f, q_norm_w, k_norm_w, conv_w, conv_b, dt_bias,
                          A_log, D_skip, ssm_norm_w, w_pa, w_pb, w_out, norm2_w, w_up, w_down)
        yp, st_p = _layer(yp, None, p, yp.shape[1])
        n_past = cache_k.shape[2]
        past = (cache_k[l].reshape(dec_batch, n_past, ATT_WIDTH),
                cache_v[l].reshape(dec_batch, n_past, ATT_WIDTH),
                cache_logf[l], state_conv[l], state_ssm[l])
        ys_l, st_s = _layer(ys, past, p, dec_seq)
        states_p.append(st_p)
        states_s.append(st_s)
        if l + 1 < depth:
            ys = jnp.pad(ys_l, ((0, 0), (0, dec_pad - dec_seq), (0, 0)))
        else:
            ys = ys_l
    stack = lambda states, i: jnp.stack([s[i] for s in states], axis=0)
    return (yp, ys,
            stack(states_p, 0), stack(states_p, 1), stack(states_p, 2), stack(states_p, 3),
            stack(states_p, 4),
            stack(states_s, 0), stack(states_s, 1), stack(states_s, 2), stack(states_s, 3),
            stack(states_s, 4))
```

```python
import functools

import jax
import jax.numpy as jnp
from jax import lax
from jax.experimental import pallas as pl
from jax.experimental.pallas import tpu as pltpu

F32 = jnp.float32
BF16 = jnp.bfloat16

D_MODEL = 1024
N_HEADS_ATT = 16
HEAD_DIM = 64
ATT_WIDTH = N_HEADS_ATT * HEAD_DIM
ATT_SCALE = HEAD_DIM ** -0.5
D_INNER = 2048
SSM_HEAD_DIM = 64
N_HEADS_SSM = D_INNER // SSM_HEAD_DIM
N_GROUPS = 8
HEADS_PER_GROUP = N_HEADS_SSM // N_GROUPS
GROUP_WIDTH = D_INNER // N_GROUPS
D_STATE = 128
CONV_W = 4
CONV_DIM = D_INNER + 2 * N_GROUPS * D_STATE
D_FF = 4 * D_MODEL
EPS = 1e-6
SSD_CHUNK = 64

LANES = 128
SUBLANES = 8
NEG_BIG = -1e30
LOG2E = 1.4426950408889634
VMEM_LIMIT = 56 * 1024 * 1024

SMALL_F0 = 0
SMALL_DT0 = N_HEADS_ATT


def _split3(x):
    hi = x.astype(BF16)
    r1 = x - hi.astype(F32)
    mid = r1.astype(BF16)
    lo = (r1 - mid.astype(F32)).astype(BF16)
    return hi, mid, lo


def _dot(a, b):
    return jnp.dot(a, b, preferred_element_type=F32)


def _dot3(lhs_bf16, x):
    hi, mid, lo = _split3(x)
    return _dot(lhs_bf16, hi) + _dot(lhs_bf16, mid) + _dot(lhs_bf16, lo)


def _sigmoid(x):
    return 0.5 * jnp.tanh(0.5 * x) + 0.5


def _silu(x):
    h = 0.5 * x
    return h + h * jnp.tanh(h)


def _softplus(x):
    return jnp.maximum(x, 0.0) + jnp.log1p(jnp.exp(-jnp.abs(x)))


def _log_sigmoid(x):
    return jnp.minimum(x, 0.0) - jnp.log1p(jnp.exp(-jnp.abs(x)))


def _lane_iota(shape):
    return lax.broadcasted_iota(jnp.int32, shape, len(shape) - 1)


def _row_iota(shape):
    return lax.broadcasted_iota(jnp.int32, shape, len(shape) - 2)


IN_TN = 512
IN_QK_TILES = 2 * ATT_WIDTH // IN_TN
IN_V_TILES = ATT_WIDTH // IN_TN
IN_ZXG_TILE0 = IN_QK_TILES + IN_V_TILES
IN_ZXG_TILES = (D_INNER + CONV_DIM + 2 * D_MODEL) // IN_TN
IN_MAIN_TILES = IN_ZXG_TILE0 + IN_ZXG_TILES


def _inproj_kernel(x_ref, nw_ref, wm_ref, ws_ref, qk_ref, v_ref, zxg_ref, small_ref, xn_ref):
    j = pl.program_id(1)

    @pl.when(j == 0)
    def _():
        x = x_ref[...]
        ms = jnp.mean(x * x, axis=-1, keepdims=True)
        xn = (x * lax.rsqrt(ms + EPS)) * nw_ref[...]
        xn_ref[...] = xn.astype(BF16)
        small_ref[...] = _dot(xn_ref[...], ws_ref[...])

    @pl.when(j < IN_QK_TILES)
    def _():
        qk_ref[...] = _dot(xn_ref[...], wm_ref[...])

    @pl.when((j >= IN_QK_TILES) & (j < IN_ZXG_TILE0))
    def _():
        v_ref[...] = _dot(xn_ref[...], wm_ref[...])

    @pl.when(j >= IN_ZXG_TILE0)
    def _():
        zxg_ref[...] = _dot(xn_ref[...], wm_ref[...])


def _inproj(x2d, nw, w_main, w_small, tm):
    rows = x2d.shape[0]
    grid = (rows // tm, IN_MAIN_TILES)
    n_zxg = IN_ZXG_TILES
    return pl.pallas_call(
        _inproj_kernel,
        grid=grid,
        in_specs=[
            pl.BlockSpec((tm, D_MODEL), lambda i, j: (i, 0)),
            pl.BlockSpec((1, D_MODEL), lambda i, j: (0, 0)),
            pl.BlockSpec((D_MODEL, IN_TN), lambda i, j: (0, j)),
            pl.BlockSpec((D_MODEL, LANES), lambda i, j: (0, 0)),
        ],
        out_specs=[
            pl.BlockSpec((tm, IN_TN), lambda i, j: (i, jnp.minimum(j, IN_QK_TILES - 1))),
            pl.BlockSpec((tm, IN_TN), lambda i, j: (i, jnp.clip(j - IN_QK_TILES, 0, IN_V_TILES - 1))),
            pl.BlockSpec((tm, IN_TN), lambda i, j: (i, jnp.clip(j - IN_ZXG_TILE0, 0, n_zxg - 1))),
            pl.BlockSpec((tm, LANES), lambda i, j: (i, 0)),
        ],
        out_shape=[
            jax.ShapeDtypeStruct((rows, IN_QK_TILES * IN_TN), F32),
            jax.ShapeDtypeStruct((rows, IN_V_TILES * IN_TN), F32),
            jax.ShapeDtypeStruct((rows, n_zxg * IN_TN), F32),
            jax.ShapeDtypeStruct((rows, LANES), F32),
        ],
        scratch_shapes=[pltpu.VMEM((tm, D_MODEL), BF16)],
        compiler_params=pltpu.CompilerParams(
            dimension_semantics=("arbitrary", "arbitrary"), vmem_limit_bytes=VMEM_LIMIT),
        name="inproj",
    )(x2d, nw, w_main, w_small)


AUG0 = HEAD_DIM
AUG_SHIFT = AUG0 + 6
AUG_W = LANES // N_HEADS_ATT
CP_ONE = 3 * N_HEADS_ATT
CP_SHIFT = CP_ONE + 1
SHIFT_MAX = 45.0
SHIFT_MARGIN = 1.03
VT_ROWS = HEAD_DIM + 16


def _cum_logf(logf128, carry_ref, ltri_ref):
    tp = logf128.shape[0]
    c = carry_ref[0:1, :] + _dot3(ltri_ref[...], logf128)
    carry_ref[...] = jnp.broadcast_to(c[tp - 1:tp, :], carry_ref.shape)
    return c


def _c_parts(c, shift=None):
    hi, mid, lo = _split3(c)
    lane = _lane_iota(c.shape)
    h = N_HEADS_ATT
    tail = jnp.where(lane == CP_ONE, 1.0, 0.0)
    if shift is not None:
        tail = jnp.where(lane == CP_SHIFT, shift, tail)
    packed = jnp.where(
        lane < h, hi.astype(F32),
        jnp.where(lane < 2 * h, pltpu.roll(mid.astype(F32), h, 1),
                  jnp.where(lane < 3 * h, pltpu.roll(lo.astype(F32), 2 * h, 1), tail)))
    return packed.astype(BF16)


def _head_rows(x, extras, out_ref, scale, zero_tail):
    lane = _lane_iota((x.shape[0], LANES))
    for hp in range(N_HEADS_ATT // 2):
        pair = x[:, LANES * hp:LANES * (hp + 1)] * scale
        for o in range(2):
            h = 2 * hp + o
            xh = pair if o == 0 else pltpu.roll(pair, HEAD_DIM, 1)
            ex = pltpu.roll(extras, (AUG0 - AUG_W * h) % LANES, 1)
            if zero_tail:
                ex = jnp.where(lane < AUG0 + AUG_W, ex, 0.0)
            out_ref[0, h] = jnp.where(lane < HEAD_DIM, xh, ex).astype(BF16)


def _write_vt(v, vt_ref):
    vt = v.T
    tail_shape = (VT_ROWS - HEAD_DIM, v.shape[0])
    tail = jnp.where(_row_iota(tail_shape) == 0, 1.0, 0.0).astype(BF16)
    for h in range(N_HEADS_ATT):
        vt_ref[0, h, 0:HEAD_DIM, :] = vt[HEAD_DIM * h:HEAD_DIM * (h + 1), :].astype(BF16)
        vt_ref[0, h, HEAD_DIM:VT_ROWS, :] = tail


def _head_rmsnorm(x, g_ref, gt_ref, w):
    sq = x * x
    hi = sq.astype(BF16)
    lo = (sq - hi.astype(F32)).astype(BF16)
    ssq = _dot(hi, g_ref[...]) + _dot(lo, g_ref[...])
    rs = lax.rsqrt(ssq * (1.0 / HEAD_DIM) + EPS)
    rs_hi = rs.astype(BF16)
    rs_lo = (rs - rs_hi.astype(F32)).astype(BF16)
    rs_e = _dot(rs_hi, gt_ref[...]) + _dot(rs_lo, gt_ref[...])
    return (x * rs_e) * w


def _prep_new_kernel(q_ref, k_ref, v_ref, small_ref, c0_ref, shift_ref, bf_ref, qw_ref, kw_ref,
                     g_ref, gt_ref, selq_ref, selk_ref, ltri_ref, *rest, aliased):
    if aliased:
        rest = rest[2:]
    kout_ref, logf_ref, qaug_ref, kaug_ref, vt_ref, cend_ref, carry_ref = rest
    t = pl.program_id(1)

    @pl.when(t == 0)
    def _():
        carry_ref[...] = c0_ref[0]

    logf = _log_sigmoid(small_ref[...] + bf_ref[...])
    logf_ref[...] = logf[:, SMALL_F0:SMALL_F0 + N_HEADS_ATT]
    c = _cum_logf(logf, carry_ref, ltri_ref)
    cend_ref[0] = carry_ref[...]
    cp = _c_parts(c * LOG2E, shift_ref[0, 0:1, :])

    qn = _head_rmsnorm(q_ref[...], g_ref, gt_ref, qw_ref[...])
    _head_rows(qn, _dot(cp, selq_ref[...]), qaug_ref, ATT_SCALE * LOG2E, False)
    kn = _head_rmsnorm(k_ref[...], g_ref, gt_ref, kw_ref[...])
    _head_rows(kn, _dot(cp, selk_ref[...]), kaug_ref, 1.0, True)
    kout_ref[...] = kn
    _write_vt(v_ref[...], vt_ref)


def _prep_cache_kernel(k_ref, v_ref, logf_ref, c0_ref, g_ref, selk_ref, ltri_ref,
                       kaug_ref, vt_ref, cend_ref, kmax_ref, carry_ref):
    t = pl.program_id(1)
    n_cache_tiles = pl.num_programs(1) - 1

    @pl.when(t == 0)
    def _():
        carry_ref[...] = c0_ref[0]
        kmax_ref[...] = jnp.zeros(kmax_ref.shape, F32)

    @pl.when(t < n_cache_tiles)
    def _():
        c = _cum_logf(logf_ref[0], carry_ref, ltri_ref)
        cend_ref[0] = carry_ref[...]
        k = k_ref[0]
        _head_rows(k, _dot(_c_parts(c * LOG2E), selk_ref[...]), kaug_ref, 1.0, True)
        _write_vt(v_ref[0], vt_ref)
        sq = k * k
        hi = sq.astype(BF16)
        lo = (sq - hi.astype(F32)).astype(BF16)
        ssq = _dot(hi, g_ref[...]) + _dot(lo, g_ref[...])
        kmax_ref[0] = jnp.maximum(kmax_ref[0], jnp.max(ssq, axis=0, keepdims=True))

    @pl.when(t == n_cache_tiles)
    def _():
        kaug_ref[...] = jnp.zeros(kaug_ref.shape, BF16)
        vt_ref[...] = jnp.zeros(vt_ref.shape, BF16)


def _aug_selectors():
    h = N_HEADS_ATT
    rows = jnp.arange(LANES)[:, None]
    cols = jnp.arange(LANES)[None, :]
    head = cols // AUG_W
    pos = cols % AUG_W
    one_row = rows == CP_ONE
    selq = jnp.zeros((LANES, LANES), F32)
    selk = jnp.zeros((LANES, LANES), F32)
    for part in range(3):
        part_row = rows == part * h + head
        selq = selq + jnp.where(part_row & (pos == part), 1.0, 0.0)
        selq = selq + jnp.where(one_row & (pos == 3 + part), 1.0, 0.0)
        selk = selk + jnp.where(one_row & (pos == part), 1.0, 0.0)
        selk = selk - jnp.where(part_row & (pos == 3 + part), 1.0, 0.0)
    selq = selq - jnp.where((rows == CP_SHIFT) & (pos == AUG_SHIFT - AUG0), 1.0, 0.0)
    selk = selk + jnp.where(one_row & (pos == AUG_SHIFT - AUG0), 1.0, 0.0)
    return selq.astype(BF16), selk.astype(BF16)


def _head_group_matrices():
    j = jnp.arange(ATT_WIDTH)[:, None] // HEAD_DIM
    h = jnp.arange(LANES)[None, :]
    g = jnp.where(j == h, 1.0, 0.0).astype(BF16)
    return g, g.T


def _ltri(n, block=None):
    r = jnp.arange(n)[:, None]
    c = jnp.arange(n)[None, :]
    keep = c <= r
    if block is not None:
        keep = keep & (r // block == c // block)
    return jnp.where(keep, 1.0, 0.0).astype(BF16)


def _prep_new(qk, v, small, c0, shift, bf128, qw, kw, batch, seq, tk, bufs=None, t_off_blocks=0):
    nt = seq // tk
    h = N_HEADS_ATT
    g, gt = _head_group_matrices()
    selq, selk = _aug_selectors()
    ltri = _ltri(tk)
    rows = batch * seq
    const = lambda shape: pl.BlockSpec(shape, lambda b, t: tuple(0 for _ in shape))
    in_specs = [
        pl.BlockSpec((tk, ATT_WIDTH), lambda b, t: (b * nt + t, 0)),
        pl.BlockSpec((tk, ATT_WIDTH), lambda b, t: (b * nt + t, 1)),
        pl.BlockSpec((tk, ATT_WIDTH), lambda b, t: (b * nt + t, 0)),
        pl.BlockSpec((tk, LANES), lambda b, t: (b * nt + t, 0)),
        pl.BlockSpec((1, SUBLANES, LANES), lambda b, t: (b, 0, 0)),
        pl.BlockSpec((1, SUBLANES, LANES), lambda b, t: (b, 0, 0)),
        const((1, LANES)), const((1, ATT_WIDTH)), const((1, ATT_WIDTH)),
        const((ATT_WIDTH, LANES)), const((LANES, ATT_WIDTH)),
        const((LANES, LANES)), const((LANES, LANES)), const((tk, tk)),
    ]
    args = [qk, qk, v, small, c0, shift, bf128, qw, kw, g, gt, selq, selk, ltri]
    if bufs is None:
        kv_len = seq
        aliases = {}
    else:
        kv_len = bufs[0].shape[2]
        in_specs += [pl.BlockSpec(memory_space=pl.ANY), pl.BlockSpec(memory_space=pl.ANY)]
        aliases = {len(args): 3, len(args) + 1: 4}
        args += list(bufs)
    o = t_off_blocks
    out_specs = [
        pl.BlockSpec((tk, ATT_WIDTH), lambda b, t: (b * nt + t, 0)),
        pl.BlockSpec((tk, h), lambda b, t: (b * nt + t, 0)),
        pl.BlockSpec((1, h, tk, LANES), lambda b, t: (b, 0, t, 0)),
        pl.BlockSpec((1, h, tk, LANES), lambda b, t: (b, 0, t + o, 0)),
        pl.BlockSpec((1, h, VT_ROWS, tk), lambda b, t: (b, 0, 0, t + o)),
        pl.BlockSpec((1, SUBLANES, LANES), lambda b, t: (b, 0, 0)),
    ]
    out_shape = [
        jax.ShapeDtypeStruct((rows, ATT_WIDTH), F32),
        jax.ShapeDtypeStruct((rows, h), F32),
        jax.ShapeDtypeStruct((batch, h, seq, LANES), BF16),
        jax.ShapeDtypeStruct((batch, h, kv_len, LANES), BF16),
        jax.ShapeDtypeStruct((batch, h, VT_ROWS, kv_len), BF16),
        jax.ShapeDtypeStruct((batch, SUBLANES, LANES), F32),
    ]
    return pl.pallas_call(
        functools.partial(_prep_new_kernel, aliased=bufs is not None),
        grid=(batch, nt),
        in_specs=in_specs,
        out_specs=out_specs,
        out_shape=out_shape,
        scratch_shapes=[pltpu.VMEM((SUBLANES, LANES), F32)],
        input_output_aliases=aliases,
        compiler_params=pltpu.CompilerParams(
            dimension_semantics=("arbitrary", "arbitrary"), vmem_limit_bytes=VMEM_LIMIT),
        name="prep_new",
    )(*args)


def _prep_cache(cache_k, cache_v, cache_logf128, c0, tk, kv_len):
    batch, past, _ = cache_k.shape
    nt = past // tk
    h = N_HEADS_ATT
    g, _ = _head_group_matrices()
    _, selk = _aug_selectors()
    ltri = _ltri(tk)
    const = lambda shape: pl.BlockSpec(shape, lambda b, t: tuple(0 for _ in shape))
    assert kv_len - past <= tk
    last = nt - 1
    return pl.pallas_call(
        _prep_cache_kernel,
        grid=(batch, nt + 1),
        in_specs=[
            pl.BlockSpec((1, tk, ATT_WIDTH), lambda b, t: (b, jnp.minimum(t, last), 0)),
            pl.BlockSpec((1, tk, ATT_WIDTH), lambda b, t: (b, jnp.minimum(t, last), 0)),
            pl.BlockSpec((1, tk, LANES), lambda b, t: (b, jnp.minimum(t, last), 0)),
            pl.BlockSpec((1, SUBLANES, LANES), lambda b, t: (b, 0, 0)),
            const((ATT_WIDTH, LANES)), const((LANES, LANES)), const((tk, tk)),
        ],
        out_specs=[
            pl.BlockSpec((1, h, tk, LANES), lambda b, t: (b, 0, t, 0)),
            pl.BlockSpec((1, h, VT_ROWS, tk), lambda b, t: (b, 0, 0, t)),
            pl.BlockSpec((1, SUBLANES, LANES), lambda b, t: (b, 0, 0)),
            pl.BlockSpec((1, 1, LANES), lambda b, t: (b, 0, 0)),
        ],
        out_shape=[
            jax.ShapeDtypeStruct((batch, h, kv_len, LANES), BF16),
            jax.ShapeDtypeStruct((batch, h, VT_ROWS, kv_len), BF16),
            jax.ShapeDtypeStruct((batch, SUBLANES, LANES), F32),
            jax.ShapeDtypeStruct((batch, 1, LANES), F32),
        ],
        scratch_shapes=[pltpu.VMEM((SUBLANES, LANES), F32)],
        compiler_params=pltpu.CompilerParams(
            dimension_semantics=("arbitrary", "arbitrary"), vmem_limit_bytes=VMEM_LIMIT),
        name="prep_cache",
    )(cache_k, cache_v, cache_logf128, c0, g, selk, ltri)


def _flash_kernel(q_ref, k_ref, vt_ref, o_ref, s0_ref, s1_ref, cm_ref, m_ref, acc_ref,
                  *, tq, tkf, q_off):
    qi = pl.program_id(2)
    q_start = q_off + qi * tq
    n_full = q_start // tkf
    acc_ref[...] = jnp.zeros(acc_ref.shape, F32)
    m_ref[...] = jnp.full(m_ref.shape, NEG_BIG, F32)
    bufs = (s0_ref, s1_ref)

    def scores(k_start, rows, masked, slot):
        for o in range(2):
            kc = k_ref[0, o, pl.ds(k_start, rows), :]
            st = lax.dot_general(kc, q_ref[0, o], (((1,), (1,)), ((), ())),
                                 preferred_element_type=F32)
            if masked:
                st = jnp.where(_row_iota(st.shape) <= _lane_iota(st.shape), st, NEG_BIG)
            bufs[slot][o, 0:rows, :] = st
            cm_ref[slot, o] = jnp.max(st, axis=0, keepdims=True)

    def consume(k_start, rows, slot):
        for o in range(2):
            m_prev = m_ref[o]
            m_new = jnp.maximum(m_prev, cm_ref[slot, o])
            alpha = jnp.exp2(m_prev - m_new)
            p = jnp.exp2(bufs[slot][o, 0:rows, :] - m_new)
            pv = _dot(vt_ref[0, o, :, pl.ds(k_start, rows)], p.astype(BF16))
            acc_ref[o] = alpha * acc_ref[o] + pv
            m_ref[o] = m_new

    full_start = lambda f: pl.multiple_of(f * tkf, tkf)
    diag_start = pl.multiple_of(q_start, tq)
    scores(diag_start, tq, True, 0)

    @pl.when(n_full > 0)
    def _():
        scores(full_start(0), tkf, False, 1)

    consume(diag_start, tq, 0)

    pairs = jnp.maximum(n_full - 1, 0) // 2

    def body(t, carry):
        f = 2 * t
        scores(full_start(f + 1), tkf, False, 0)
        consume(full_start(f), tkf, 1)
        scores(full_start(f + 2), tkf, False, 1)
        consume(full_start(f + 1), tkf, 0)
        return carry

    lax.fori_loop(0, pairs, body, 0)
    f_rem = 2 * pairs
    left = n_full - f_rem

    @pl.when(left == 2)
    def _():
        scores(full_start(f_rem + 1), tkf, False, 0)
        consume(full_start(f_rem), tkf, 1)
        consume(full_start(f_rem + 1), tkf, 0)

    @pl.when(left == 1)
    def _():
        consume(full_start(f_rem), tkf, 1)

    outs = [acc_ref[o, 0:HEAD_DIM, :] / acc_ref[o, HEAD_DIM:HEAD_DIM + 1, :] for o in range(2)]
    o_ref[0] = jnp.concatenate(outs, axis=0).T.astype(BF16)


FLASH_TRIP_KEYS = 2048


def _flash_fixed_kernel(q_ref, k_ref, vt_ref, o_ref, pa_ref, pb_ref, acc_ref, *, tq, tkf, q_off):
    qi = pl.program_id(2)
    q_start = q_off + qi * tq
    n_full = q_start // tkf
    slot_of = lambda f_parity: pa_ref if f_parity else pb_ref

    def probs(k_start, rows, masked, p_ref):
        for o in range(2):
            kc = k_ref[0, o, pl.ds(k_start, rows), :]
            st = lax.dot_general(kc, q_ref[0, o], (((1,), (1,)), ((), ())),
                                 preferred_element_type=F32)
            if masked:
                st = jnp.where(_row_iota(st.shape) <= _lane_iota(st.shape), st, NEG_BIG)
            p_ref[o, 0:rows, :] = jnp.exp2(st).astype(BF16)

    def accumulate(k_start, rows, p_ref, first=False):
        for o in range(2):
            pv = _dot(vt_ref[0, o, :, pl.ds(k_start, rows)], p_ref[o, 0:rows, :])
            acc_ref[o] = pv if first else acc_ref[o] + pv

    full_start = lambda f: pl.multiple_of(f * tkf, tkf)
    diag_start = pl.multiple_of(q_start, tq)
    probs(diag_start, tq, True, pa_ref)

    if tq == tkf:
        start_of = lambda g: pl.multiple_of(jnp.where(g == 0, q_start, (g - 1) * tkf), tkf)
        acc_ref[...] = jnp.zeros(acc_ref.shape, F32)
        n_steps = n_full
        enabled = True

        def step(g, parity):
            probs(full_start(g), tkf, False, slot_of(parity))
            accumulate(start_of(g), tkf, slot_of(1 - parity))

        last = lambda g, parity: accumulate(start_of(g), tkf, slot_of(1 - parity))
    else:
        @pl.when(n_full > 0)
        def _():
            probs(full_start(0), tkf, False, slot_of(0))

        accumulate(diag_start, tq, pa_ref, first=True)
        n_steps = jnp.maximum(n_full - 1, 0)
        enabled = n_full > 0

        def step(f, parity):
            probs(full_start(f + 1), tkf, False, slot_of(1 - parity))
            accumulate(full_start(f), tkf, slot_of(parity))

        last = lambda f, parity: accumulate(full_start(f), tkf, slot_of(parity))

    unroll = max(2, FLASH_TRIP_KEYS // tkf)

    def body(t, carry):
        for i in range(unroll):
            step(unroll * t + i, i % 2)
        return carry

    lax.fori_loop(0, n_steps // unroll, body, 0)
    f0 = (n_steps // unroll) * unroll
    left = n_steps - f0
    for r in range(unroll):
        @pl.when(enabled & (left == r))
        def _(r=r):
            for i in range(r):
                step(f0 + i, i % 2)
            last(f0 + r, r % 2)

    outs = [acc_ref[o, 0:HEAD_DIM, :] / acc_ref[o, HEAD_DIM:HEAD_DIM + 1, :] for o in range(2)]
    o_ref[0] = jnp.concatenate(outs, axis=0).T.astype(BF16)


def _flash(q_aug, k_aug, vt, tq, tkf, q_off, fixed_reference):
    batch, h, tq_total, _ = q_aug.shape
    kv_len = k_aug.shape[2]
    nq = tq_total // tq
    if fixed_reference:
        body = _flash_fixed_kernel
        scratch = [
            pltpu.VMEM((2, max(tq, tkf), tq), BF16),
            pltpu.VMEM((2, max(tq, tkf), tq), BF16),
            pltpu.VMEM((2, VT_ROWS, tq), F32),
        ]
    else:
        body = _flash_kernel
        scratch = [
            pltpu.VMEM((2, max(tq, tkf), tq), F32),
            pltpu.VMEM((2, max(tq, tkf), tq), F32),
            pltpu.VMEM((2, 2, 1, tq), F32),
            pltpu.VMEM((2, 1, tq), F32),
            pltpu.VMEM((2, VT_ROWS, tq), F32),
        ]
    return pl.pallas_call(
        functools.partial(body, tq=tq, tkf=tkf, q_off=q_off),
        grid=(batch, h // 2, nq),
        in_specs=[
            pl.BlockSpec((1, 2, tq, LANES), lambda b, hp, qi: (b, hp, qi, 0)),
            pl.BlockSpec((1, 2, kv_len, LANES), lambda b, hp, qi: (b, hp, 0, 0)),
            pl.BlockSpec((1, 2, VT_ROWS, kv_len), lambda b, hp, qi: (b, hp, 0, 0)),
        ],
        out_specs=pl.BlockSpec((1, tq, LANES), lambda b, hp, qi: (b, qi, hp)),
        out_shape=jax.ShapeDtypeStruct((batch, tq_total, ATT_WIDTH), BF16),
        scratch_shapes=scratch,
        compiler_params=pltpu.CompilerParams(
            dimension_semantics=("arbitrary", "arbitrary", "arbitrary"),
            vmem_limit_bytes=VMEM_LIMIT),
        name="flash_fixed" if fixed_reference else "flash",
    )(q_aug, k_aug, vt)


B_LANE0 = D_INNER
C_LANE0 = D_INNER + N_GROUPS * D_STATE
PK_HI0 = SMALL_DT0
PK_MID0 = PK_HI0 + N_HEADS_SSM
PK_LO0 = PK_MID0 + N_HEADS_SSM


def _pack_head_parts(x):
    hi, mid, lo = _split3(x)
    lane = _lane_iota(x.shape)
    packed = jnp.where(
        lane < PK_MID0, hi.astype(F32),
        jnp.where(lane < PK_LO0, pltpu.roll(mid.astype(F32), N_HEADS_SSM, 1),
                  pltpu.roll(lo.astype(F32), 2 * N_HEADS_SSM, 1)))
    return packed.astype(BF16)


def _head_expand_matrix():
    rows = jnp.arange(LANES)[:, None]
    head = jnp.arange(D_INNER)[None, :] // SSM_HEAD_DIM
    hit = (rows == PK_HI0 + head) | (rows == PK_MID0 + head) | (rows == PK_LO0 + head)
    return jnp.where(hit, 1.0, 0.0).astype(BF16)


def _ssd_kernel(z_ref, xs_ref, bc_ref, small_ref, cpast_ref, h0_ref, cw_ref, cb_ref,
                dtb_ref, alog_ref, dskip_ref, nw_ref, lblk_ref, sel_ref,
                y_ref, cnew_ref, hlast_ref,
                xbuf_ref, ht_ref, ae_ref, dte_ref, ybuf_ref, *, tb, n_valid, tail_row):
    t = pl.program_id(1)
    nt = pl.num_programs(1)
    n_state_blocks = D_INNER // LANES

    n_conv_slabs = CONV_DIM // LANES
    n_inner_slabs = D_INNER // LANES

    @pl.when(t == 0)
    def _():
        for j in range(n_conv_slabs):
            xbuf_ref[j, 0:SUBLANES, :] = cpast_ref[0, :, LANES * j:LANES * (j + 1)]
        for i in range(n_state_blocks):
            ht_ref[:, LANES * i:LANES * (i + 1)] = h0_ref[0, LANES * i:LANES * (i + 1), :].T

    for j in range(n_inner_slabs):
        xbuf_ref[j, SUBLANES:SUBLANES + tb, :] = xs_ref[:, LANES * j:LANES * (j + 1)]
        xbuf_ref[n_inner_slabs + j, SUBLANES:SUBLANES + tb, :] = bc_ref[:, LANES * j:LANES * (j + 1)]

    lane = _lane_iota((tb, LANES))
    row = _row_iota((tb, LANES)) + t * tb
    live = (lane >= SMALL_DT0) & (lane < SMALL_DT0 + N_HEADS_SSM) & (row < n_valid)
    dt = jnp.where(live, _softplus(small_ref[...] + dtb_ref[...]), 0.0)
    a = dt * (-jnp.exp(alog_ref[...]))
    acum = _dot3(lblk_ref[...], a)
    ae_all = _dot(_pack_head_parts(acum), sel_ref[...])
    dte_all = _dot(_pack_head_parts(dt), sel_ref[...])
    for j in range(n_inner_slabs):
        ae_ref[j] = ae_all[:, LANES * j:LANES * (j + 1)]
        dte_ref[j] = dte_all[:, LANES * j:LANES * (j + 1)]

    gw = GROUP_WIDTH
    token_of = lambda r: 32 * (r // 32) + (r // SUBLANES) % 4 + 4 * (r % SUBLANES)
    sub = _row_iota((SSD_CHUNK, gw))
    pos = _lane_iota((SSD_CHUNK, gw)) % SSM_HEAD_DIM
    diag = pos == sub
    causal = token_of(pos) <= token_of(sub)
    blk_row = _row_iota((gw, gw)) // SSM_HEAD_DIM
    blk_col = _lane_iota((gw, gw)) // SSM_HEAD_DIM
    same_head = blk_row == blk_col
    windows = [(32 * half + a) for half in range(2) for a in range(4)]

    def load_perm(ref, j, row0):
        return jnp.concatenate(
            [ref[j, pl.ds(row0 + w, SUBLANES, stride=4), :] for w in windows], axis=0)

    def conv_slab(c0, j):
        cols = slice(LANES * j, LANES * (j + 1))
        out = cb_ref[:, cols]
        for i in range(CONV_W):
            out = out + load_perm(xbuf_ref, j, c0 + SUBLANES - (CONV_W - 1) + i) * cw_ref[i:i + 1, cols]
        return _silu(out)

    def chunk(c, carry):
        c0 = pl.multiple_of(c * SSD_CHUNK, SSD_CHUNK)
        rows = pl.ds(c0, SSD_CHUNK)
        for g in range(N_GROUPS):
            lanes = slice(gw * g, gw * (g + 1))
            j0 = 2 * g
            xs = jnp.concatenate([conv_slab(c0, j0), conv_slab(c0, j0 + 1)], axis=1)
            bm = conv_slab(c0, n_inner_slabs + g)
            cm = conv_slab(c0, n_inner_slabs + N_GROUPS + g)
            ae = jnp.concatenate([load_perm(ae_ref, j0, c0), load_perm(ae_ref, j0 + 1, c0)], axis=1)
            dte = jnp.concatenate([load_perm(dte_ref, j0, c0), load_perm(dte_ref, j0 + 1, c0)], axis=1)
            xdt = xs * dte
            bm16 = bm.astype(BF16)
            cm16 = cm.astype(BF16)

            b4 = jnp.concatenate([bm16] * HEADS_PER_GROUP, axis=0)
            cb4 = lax.dot_general(cm16, b4, (((1,), (1,)), ((), ())), preferred_element_type=F32)
            a_row = jnp.sum(jnp.where(diag, ae, 0.0), axis=0, keepdims=True)
            decay = jnp.exp(jnp.where(causal, ae - a_row, NEG_BIG))
            xdt16 = xdt.astype(BF16)
            x4 = jnp.where(same_head, jnp.concatenate([xdt16] * HEADS_PER_GROUP, axis=0),
                           jnp.zeros((), BF16))
            y = _dot((cb4 * decay).astype(BF16), x4)

            hprev = ht_ref[:, lanes]
            y = y + jnp.exp(ae) * _dot(cm16, hprev.astype(BF16))
            a_last = ae[SSD_CHUNK - 1:SSD_CHUNK, :]
            xw = (xdt * jnp.exp(a_last - ae)).astype(BF16)
            upd = lax.dot_general(bm16, xw, (((0,), (0,)), ((), ())), preferred_element_type=F32)
            ht_ref[:, lanes] = hprev * jnp.exp(a_last) + upd

            y = y + xs * dskip_ref[:, lanes]
            for jj in range(2):
                for v, w in enumerate(windows):
                    ybuf_ref[jj, pl.ds(c0 + w, SUBLANES, stride=4), :] = (
                        y[SUBLANES * v:SUBLANES * (v + 1), LANES * jj:LANES * (jj + 1)])
            y = jnp.concatenate([ybuf_ref[0, rows, :], ybuf_ref[1, rows, :]], axis=1)
            y = y * _silu(z_ref[rows, lanes])
            ms = jnp.mean(y * y, axis=-1, keepdims=True)
            y_ref[rows, lanes] = ((y * lax.rsqrt(ms + EPS)) * nw_ref[:, lanes]).astype(BF16)
        return carry

    lax.fori_loop(0, tb // SSD_CHUNK, chunk, 0)

    for j in range(n_conv_slabs):
        cnew_ref[0, :, LANES * j:LANES * (j + 1)] = xbuf_ref[j, tail_row:tail_row + SUBLANES, :]
        xbuf_ref[j, 0:SUBLANES, :] = xbuf_ref[j, tb:tb + SUBLANES, :]

    @pl.when(t == nt - 1)
    def _():
        for i in range(n_state_blocks):
            hlast_ref[0, LANES * i:LANES * (i + 1), :] = ht_ref[:, LANES * i:LANES * (i + 1)].T


def _ssd(zxg, small, conv_past8, h0, cw, cb, dtb128, alog128, dskip_e, nw, batch, seq, tb, n_valid):
    nt = seq // tb
    rows = batch * seq
    tail_row = n_valid - (seq - tb)
    lblk = _ltri(tb, SSD_CHUNK)
    sel = _head_expand_matrix()
    const = lambda shape: pl.BlockSpec(shape, lambda b, t: tuple(0 for _ in shape))
    return pl.pallas_call(
        functools.partial(_ssd_kernel, tb=tb, n_valid=n_valid, tail_row=tail_row),
        grid=(batch, nt),
        in_specs=[
            pl.BlockSpec((tb, D_INNER), lambda b, t: (b * nt + t, 0)),
            pl.BlockSpec((tb, D_INNER), lambda b, t: (b * nt + t, 1)),
            pl.BlockSpec((tb, D_INNER), lambda b, t: (b * nt + t, 2)),
            pl.BlockSpec((tb, LANES), lambda b, t: (b * nt + t, 0)),
            pl.BlockSpec((1, SUBLANES, CONV_DIM), lambda b, t: (b, 0, 0)),
            pl.BlockSpec((1, D_INNER, D_STATE), lambda b, t: (b, 0, 0)),
            const((CONV_W, CONV_DIM)), const((1, CONV_DIM)),
            const((1, LANES)), const((1, LANES)), const((1, D_INNER)), const((1, D_INNER)),
            const((tb, tb)), const((LANES, D_INNER)),
        ],
        out_specs=[
            pl.BlockSpec((tb, D_INNER), lambda b, t: (b * nt + t, 0)),
            pl.BlockSpec((1, SUBLANES, CONV_DIM), lambda b, t: (b, 0, 0)),
            pl.BlockSpec((1, D_INNER, D_STATE), lambda b, t: (b, 0, 0)),
        ],
        out_shape=[
            jax.ShapeDtypeStruct((rows, D_INNER), BF16),
            jax.ShapeDtypeStruct((batch, SUBLANES, CONV_DIM), F32),
            jax.ShapeDtypeStruct((batch, D_INNER, D_STATE), F32),
        ],
        scratch_shapes=[
            pltpu.VMEM((CONV_DIM // LANES, tb + SUBLANES, LANES), F32),
            pltpu.VMEM((D_STATE, D_INNER), F32),
            pltpu.VMEM((D_INNER // LANES, tb, LANES), F32),
            pltpu.VMEM((D_INNER // LANES, tb, LANES), F32),
            pltpu.VMEM((2, tb, LANES), F32),
        ],
        compiler_params=pltpu.CompilerParams(
            dimension_semantics=("arbitrary", "arbitrary"), vmem_limit_bytes=VMEM_LIMIT),
        name="ssd",
    )(zxg, zxg, zxg, small, conv_past8, h0, cw, cb, dtb128, alog128, dskip_e, nw, lblk, sel)


FF_CHUNK = 1024


def _merge_kernel(att_ref, ys_ref, ga_ref, gs_ref, x_ref, wpa_ref, wpb_ref, wout_ref,
                  n2_ref, wup_ref, wdn_ref, o_ref):
    merged = (_sigmoid(ga_ref[...]) * _dot(att_ref[...], wpa_ref[...])
              + _sigmoid(gs_ref[...]) * _dot(ys_ref[...], wpb_ref[...]))
    x1 = x_ref[...] + _dot(merged.astype(BF16), wout_ref[...])
    ms = jnp.mean(x1 * x1, axis=-1, keepdims=True)
    hn = ((x1 * lax.rsqrt(ms + EPS)) * n2_ref[...]).astype(BF16)
    acc = x1
    for c in range(D_FF // FF_CHUNK):
        cols = slice(FF_CHUNK * c, FF_CHUNK * (c + 1))
        up = jnp.maximum(_dot(hn, wup_ref[:, cols]), 0.0)
        acc = acc + _dot((up * up).astype(BF16), wdn_ref[cols, :])
    o_ref[...] = acc


def _merge(att, ys, zxg, x2d, wpa, wpb, wout, n2, wup, wdn, tm):
    rows = x2d.shape[0]
    ga_blk = 6
    gs_blk = 7
    resident = lambda shape: pl.BlockSpec(shape, lambda i: tuple(0 for _ in shape),
                                          pipeline_mode=pl.Buffered(1))
    return pl.pallas_call(
        _merge_kernel,
        grid=(rows // tm,),
        in_specs=[
            pl.BlockSpec((tm, ATT_WIDTH), lambda i: (i, 0)),
            pl.BlockSpec((tm, D_INNER), lambda i: (i, 0)),
            pl.BlockSpec((tm, D_MODEL), lambda i: (i, ga_blk)),
            pl.BlockSpec((tm, D_MODEL), lambda i: (i, gs_blk)),
            pl.BlockSpec((tm, D_MODEL), lambda i: (i, 0)),
            resident((ATT_WIDTH, D_MODEL)), resident((D_INNER, D_MODEL)),
            resident((D_MODEL, D_MODEL)), resident((1, D_MODEL)),
            resident((D_MODEL, D_FF)), resident((D_FF, D_MODEL)),
        ],
        out_specs=pl.BlockSpec((tm, D_MODEL), lambda i: (i, 0)),
        out_shape=jax.ShapeDtypeStruct((rows, D_MODEL), F32),
        compiler_params=pltpu.CompilerParams(
            dimension_semantics=("arbitrary",), vmem_limit_bytes=VMEM_LIMIT),
        name="merge",
    )(att, ys, zxg, zxg, x2d, wpa, wpb, wout, n2, wup, wdn)


def _pick_tile(n, prefs):
    for p in prefs:
        if n % p == 0:
            return p
    raise ValueError(f"no tile in {prefs} divides {n}")


def _pad_lanes(vec, lane0):
    return jnp.zeros((1, LANES), F32).at[0, lane0:lane0 + vec.shape[0]].set(vec.astype(F32))


def _layer(x, past, p, n_valid):
    batch, seq, _ = x.shape
    rows = batch * seq
    x2d = x.reshape(rows, D_MODEL)

    tm = _pick_tile(rows, (2048, 1024, 512, 256, 128))
    qk, v, zxg, small = _inproj(x2d, p["norm1_w"], p["w_main"], p["w_small"], tm)

    tk = _pick_tile(seq, (512, 256, 128)) if past is None else LANES
    zeros_c = jnp.zeros((batch, SUBLANES, LANES), F32)
    k2_new = HEAD_DIM * jnp.max(jnp.square(p["kw"]))
    if past is None:
        k2 = jnp.full((batch,), k2_new, F32)
    else:
        cache_k, cache_v, cache_logf, conv_past, h0 = past
        n_past = cache_k.shape[1]
        kv_len = n_past + seq
        logf128 = jnp.pad(cache_logf.astype(F32), ((0, 0), (0, 0), (0, LANES - N_HEADS_ATT)))
        tkf = _pick_tile(n_past, (1024, 512, 256, 128))
        k_buf, vt_buf, c_past, k2_cache = _prep_cache(cache_k, cache_v, logf128, zeros_c, tkf, kv_len)
        k2 = jnp.maximum(k2_new, jnp.max(k2_cache, axis=(1, 2)))
    q_norm = jnp.sqrt(HEAD_DIM * jnp.max(jnp.square(p["qw"])))
    shift = (SHIFT_MARGIN * ATT_SCALE * LOG2E) * q_norm * jnp.sqrt(k2)
    shift = shift.astype(BF16).astype(F32)
    shift_rows = jnp.broadcast_to(shift[:, None, None], (batch, SUBLANES, LANES))
    if past is None:
        k_out, logf, q_aug, k_aug, vt, _ = _prep_new(
            qk, v, small, zeros_c, shift_rows, p["bf128"], p["qw"], p["kw"], batch, seq, tk)
        tkf, q_off = tk, 0
        conv_past8 = jnp.zeros((batch, SUBLANES, CONV_DIM), F32)
        h0 = jnp.zeros((batch, D_INNER, D_STATE), F32)
    else:
        k_out, logf, q_aug, k_aug, vt, _ = _prep_new(
            qk, v, small, c_past, shift_rows, p["bf128"], p["qw"], p["kw"], batch, seq, tk,
            bufs=(k_buf, vt_buf), t_off_blocks=n_past // tk)
        q_off = n_past
        conv_past8 = jnp.pad(conv_past.astype(F32), ((0, 0), (SUBLANES - (CONV_W - 1), 0), (0, 0)))
        h0 = h0.astype(F32).reshape(batch, D_INNER, D_STATE)

    att = lax.cond(jnp.max(shift) <= SHIFT_MAX,
                   lambda: _flash(q_aug, k_aug, vt, tk, tkf, q_off, True),
                   lambda: _flash(q_aug, k_aug, vt, tk, tkf, q_off, False))

    tb = _pick_tile(seq, (256, 128))
    ys, conv_new8, h_last = _ssd(zxg, small, conv_past8, h0, p["conv_w"], p["conv_b"],
                                 p["dtb128"], p["alog128"], p["dskip_e"], p["ssm_norm_w"],
                                 batch, seq, tb, n_valid)

    tmm = _pick_tile(rows, (512, 256, 128))
    y = _merge(att.reshape(rows, ATT_WIDTH), ys, zxg, x2d, p["w_pa"], p["w_pb"], p["w_out"],
               p["norm2_w"], p["w_up"], p["w_down"], tmm)

    nv = n_valid
    y = y.reshape(batch, seq, D_MODEL)[:, :nv]
    k_new = k_out.reshape(batch, seq, N_HEADS_ATT, HEAD_DIM)[:, :nv]
    v_new = v.reshape(batch, seq, N_HEADS_ATT, HEAD_DIM)[:, :nv]
    logf_new = logf.reshape(batch, seq, N_HEADS_ATT)[:, :nv]
    conv_new = conv_new8[:, SUBLANES - (CONV_W - 1):, :]
    h_new = h_last.reshape(batch, N_HEADS_SSM, SSM_HEAD_DIM, D_STATE)
    return y, (k_new, v_new, logf_new, conv_new, h_new)


def _layer_params(l, norm1_w, w_in, b_f, q_norm_w, k_norm_w, conv_w, conv_b, dt_bias, A_log,
                  D_skip, ssm_norm_w, w_pa, w_pb, w_out, norm2_w, w_up, w_down):
    w = w_in[l]
    o = 0
    seg = {}
    for name, size in (("q", ATT_WIDTH), ("k", ATT_WIDTH), ("v", ATT_WIDTH), ("f", N_HEADS_ATT),
                       ("z", D_INNER), ("xbc", CONV_DIM), ("dt", N_HEADS_SSM),
                       ("ga", D_MODEL), ("gs", D_MODEL)):
        seg[name] = w[:, o:o + size]
        o += size
    w_main = jnp.concatenate([seg[n] for n in ("q", "k", "v", "z", "xbc", "ga", "gs")], axis=1)
    w_small = jnp.zeros((D_MODEL, LANES), w.dtype)
    w_small = w_small.at[:, SMALL_F0:SMALL_F0 + N_HEADS_ATT].set(seg["f"])
    w_small = w_small.at[:, SMALL_DT0:SMALL_DT0 + N_HEADS_SSM].set(seg["dt"])
    row = lambda a: a.astype(F32).reshape(1, -1)
    return {
        "norm1_w": row(norm1_w[l]),
        "w_main": w_main.astype(BF16),
        "w_small": w_small.astype(BF16),
        "bf128": _pad_lanes(b_f[l], SMALL_F0),
        "qw": row(jnp.tile(q_norm_w[l], N_HEADS_ATT)),
        "kw": row(jnp.tile(k_norm_w[l], N_HEADS_ATT)),
        "conv_w": conv_w[l].astype(F32),
        "conv_b": row(conv_b[l]),
        "dtb128": _pad_lanes(dt_bias[l], SMALL_DT0),
        "alog128": _pad_lanes(A_log[l], SMALL_DT0),
        "dskip_e": row(jnp.repeat(D_skip[l], SSM_HEAD_DIM)),
        "ssm_norm_w": row(ssm_norm_w[l]),
        "w_pa": w_pa[l].astype(BF16),
        "w_pb": w_pb[l].astype(BF16),
        "w_out": w_out[l].astype(BF16),
        "norm2_w": row(norm2_w[l]),
        "w_up": w_up[l].astype(BF16),
        "w_down": w_down[l].astype(BF16),
    }


def kernel(x_prompt, x_sample, cache_k, cache_v, cache_logf, state_conv, state_ssm, norm1_w, w_in,
           b_f, q_norm_w, k_norm_w, conv_w, conv_b, dt_bias, A_log, D_skip, ssm_norm_w, w_pa, w_pb,
           w_out, norm2_w, w_up, w_down):
    depth = w_in.shape[0]
    dec_batch, dec_seq, _ = x_sample.shape
    dec_pad = -(-dec_seq // LANES) * LANES
    yp = x_prompt
    ys = jnp.pad(x_sample, ((0, 0), (0, dec_pad - dec_seq), (0, 0)))
    states_p, states_s = [], []
    for l in range(depth):
        p = _layer_params(l, norm1_w, w_in, b_f, q_norm_w, k_norm_w, conv_w, conv_b, dt_bias,
                          A_log, D_skip, ssm_norm_w, w_pa, w_pb, w_out, norm2_w, w_up, w_down)
        yp, st_p = _layer(yp, None, p, yp.shape[1])
        n_past = cache_k.shape[2]
        past = (cache_k[l].reshape(dec_batch, n_past, ATT_WIDTH),
                cache_v[l].reshape(dec_batch, n_past, ATT_WIDTH),
                cache_logf[l], state_conv[l], state_ssm[l])
        ys_l, st_s = _layer(ys, past, p, dec_seq)
        states_p.append(st_p)
        states_s.append(st_s)
        if l + 1 < depth:
            ys = jnp.pad(ys_l, ((0, 0), (0, dec_pad - dec_seq), (0, 0)))
        else:
            ys = ys_l
    stack = lambda states, i: jnp.stack([s[i] for s in states], axis=0)
    return (yp, ys,
            stack(states_p, 0), stack(states_p, 1), stack(states_p, 2), stack(states_p, 3),
            stack(states_p, 4),
            stack(states_s, 0), stack(states_s, 1), stack(states_s, 2), stack(states_s, 3),
            stack(states_s, 4))
```

```python
import functools

import jax
import jax.numpy as jnp
from jax import lax
from jax.experimental import pallas as pl
from jax.experimental.pallas import tpu as pltpu

F32 = jnp.float32
BF16 = jnp.bfloat16

D_MODEL = 1024
N_HEADS_ATT = 16
HEAD_DIM = 64
ATT_WIDTH = N_HEADS_ATT * HEAD_DIM
ATT_SCALE = HEAD_DIM ** -0.5
D_INNER = 2048
SSM_HEAD_DIM = 64
N_HEADS_SSM = D_INNER // SSM_HEAD_DIM
N_GROUPS = 8
HEADS_PER_GROUP = N_HEADS_SSM // N_GROUPS
GROUP_WIDTH = D_INNER // N_GROUPS
D_STATE = 128
CONV_W = 4
CONV_DIM = D_INNER + 2 * N_GROUPS * D_STATE
D_FF = 4 * D_MODEL
EPS = 1e-6
SSD_CHUNK = 64

LANES = 128
SUBLANES = 8
NEG_BIG = -1e30
LOG2E = 1.4426950408889634
VMEM_LIMIT = 56 * 1024 * 1024

SMALL_F0 = 0
SMALL_DT0 = N_HEADS_ATT


def _split3(x):
    hi = x.astype(BF16)
    r1 = x - hi.astype(F32)
    mid = r1.astype(BF16)
    lo = (r1 - mid.astype(F32)).astype(BF16)
    return hi, mid, lo


def _dot(a, b):
    return jnp.dot(a, b, preferred_element_type=F32)


def _dot3(lhs_bf16, x):
    hi, mid, lo = _split3(x)
    return _dot(lhs_bf16, hi) + _dot(lhs_bf16, mid) + _dot(lhs_bf16, lo)


def _sigmoid(x):
    return 0.5 * jnp.tanh(0.5 * x) + 0.5


def _silu(x):
    h = 0.5 * x
    return h + h * jnp.tanh(h)


def _softplus(x):
    return jnp.maximum(x, 0.0) + jnp.log1p(jnp.exp(-jnp.abs(x)))


def _log_sigmoid(x):
    return jnp.minimum(x, 0.0) - jnp.log1p(jnp.exp(-jnp.abs(x)))


def _lane_iota(shape):
    return lax.broadcasted_iota(jnp.int32, shape, len(shape) - 1)


def _row_iota(shape):
    return lax.broadcasted_iota(jnp.int32, shape, len(shape) - 2)


IN_TN = 512
IN_QK_TILES = 2 * ATT_WIDTH // IN_TN
IN_V_TILES = ATT_WIDTH // IN_TN
IN_ZXG_TILE0 = IN_QK_TILES + IN_V_TILES
IN_ZXG_TILES = (D_INNER + CONV_DIM + 2 * D_MODEL) // IN_TN
IN_MAIN_TILES = IN_ZXG_TILE0 + IN_ZXG_TILES


def _inproj_kernel(x_ref, nw_ref, wm_ref, ws_ref, qk_ref, v_ref, zxg_ref, small_ref, xn_ref):
    j = pl.program_id(1)

    @pl.when(j == 0)
    def _():
        x = x_ref[...]
        ms = jnp.mean(x * x, axis=-1, keepdims=True)
        xn = (x * lax.rsqrt(ms + EPS)) * nw_ref[...]
        xn_ref[...] = xn.astype(BF16)
        small_ref[...] = _dot(xn_ref[...], ws_ref[...])

    @pl.when(j < IN_QK_TILES)
    def _():
        qk_ref[...] = _dot(xn_ref[...], wm_ref[...])

    @pl.when((j >= IN_QK_TILES) & (j < IN_ZXG_TILE0))
    def _():
        v_ref[...] = _dot(xn_ref[...], wm_ref[...])

    @pl.when(j >= IN_ZXG_TILE0)
    def _():
        zxg_ref[...] = _dot(xn_ref[...], wm_ref[...])


def _inproj(x2d, nw, w_main, w_small, tm):
    rows = x2d.shape[0]
    grid = (rows // tm, IN_MAIN_TILES)
    n_zxg = IN_ZXG_TILES
    return pl.pallas_call(
        _inproj_kernel,
        grid=grid,
        in_specs=[
            pl.BlockSpec((tm, D_MODEL), lambda i, j: (i, 0)),
            pl.BlockSpec((1, D_MODEL), lambda i, j: (0, 0)),
            pl.BlockSpec((D_MODEL, IN_TN), lambda i, j: (0, j)),
            pl.BlockSpec((D_MODEL, LANES), lambda i, j: (0, 0)),
        ],
        out_specs=[
            pl.BlockSpec((tm, IN_TN), lambda i, j: (i, jnp.minimum(j, IN_QK_TILES - 1))),
            pl.BlockSpec((tm, IN_TN), lambda i, j: (i, jnp.clip(j - IN_QK_TILES, 0, IN_V_TILES - 1))),
            pl.BlockSpec((tm, IN_TN), lambda i, j: (i, jnp.clip(j - IN_ZXG_TILE0, 0, n_zxg - 1))),
            pl.BlockSpec((tm, LANES), lambda i, j: (i, 0)),
        ],
        out_shape=[
            jax.ShapeDtypeStruct((rows, IN_QK_TILES * IN_TN), F32),
            jax.ShapeDtypeStruct((rows, IN_V_TILES * IN_TN), F32),
            jax.ShapeDtypeStruct((rows, n_zxg * IN_TN), F32),
            jax.ShapeDtypeStruct((rows, LANES), F32),
        ],
        scratch_shapes=[pltpu.VMEM((tm, D_MODEL), BF16)],
        compiler_params=pltpu.CompilerParams(
            dimension_semantics=("arbitrary", "arbitrary"), vmem_limit_bytes=VMEM_LIMIT),
        name="inproj",
    )(x2d, nw, w_main, w_small)


AUG0 = HEAD_DIM
AUG_SHIFT = AUG0 + 6
AUG_W = LANES // N_HEADS_ATT
CP_ONE = 3 * N_HEADS_ATT
CP_SHIFT = CP_ONE + 1
SHIFT_MAX = 45.0
SHIFT_MARGIN = 1.03
VT_ROWS = HEAD_DIM + 16


def _cum_logf(logf128, carry_ref, ltri_ref):
    tp = logf128.shape[0]
    c = carry_ref[0:1, :] + _dot3(ltri_ref[...], logf128)
    carry_ref[...] = jnp.broadcast_to(c[tp - 1:tp, :], carry_ref.shape)
    return c


def _c_parts(c, shift=None):
    hi, mid, lo = _split3(c)
    lane = _lane_iota(c.shape)
    h = N_HEADS_ATT
    tail = jnp.where(lane == CP_ONE, 1.0, 0.0)
    if shift is not None:
        tail = jnp.where(lane == CP_SHIFT, shift, tail)
    packed = jnp.where(
        lane < h, hi.astype(F32),
        jnp.where(lane < 2 * h, pltpu.roll(mid.astype(F32), h, 1),
                  jnp.where(lane < 3 * h, pltpu.roll(lo.astype(F32), 2 * h, 1), tail)))
    return packed.astype(BF16)


def _head_rows(x, extras, out_ref, scale, zero_tail):
    lane = _lane_iota((x.shape[0], LANES))
    for hp in range(N_HEADS_ATT // 2):
        pair = x[:, LANES * hp:LANES * (hp + 1)] * scale
        for o in range(2):
            h = 2 * hp + o
            xh = pair if o == 0 else pltpu.roll(pair, HEAD_DIM, 1)
            ex = pltpu.roll(extras, (AUG0 - AUG_W * h) % LANES, 1)
            if zero_tail:
                ex = jnp.where(lane < AUG0 + AUG_W, ex, 0.0)
            out_ref[0, h] = jnp.where(lane < HEAD_DIM, xh, ex).astype(BF16)


def _write_vt(v, vt_ref):
    vt = v.T
    tail_shape = (VT_ROWS - HEAD_DIM, v.shape[0])
    tail = jnp.where(_row_iota(tail_shape) == 0, 1.0, 0.0).astype(BF16)
    for h in range(N_HEADS_ATT):
        vt_ref[0, h, 0:HEAD_DIM, :] = vt[HEAD_DIM * h:HEAD_DIM * (h + 1), :].astype(BF16)
        vt_ref[0, h, HEAD_DIM:VT_ROWS, :] = tail


def _head_rmsnorm(x, g_ref, gt_ref, w):
    sq = x * x
    hi = sq.astype(BF16)
    lo = (sq - hi.astype(F32)).astype(BF16)
    ssq = _dot(hi, g_ref[...]) + _dot(lo, g_ref[...])
    rs = lax.rsqrt(ssq * (1.0 / HEAD_DIM) + EPS)
    rs_hi = rs.astype(BF16)
    rs_lo = (rs - rs_hi.astype(F32)).astype(BF16)
    rs_e = _dot(rs_hi, gt_ref[...]) + _dot(rs_lo, gt_ref[...])
    return (x * rs_e) * w


def _prep_new_kernel(q_ref, k_ref, v_ref, small_ref, c0_ref, shift_ref, bf_ref, qw_ref, kw_ref,
                     g_ref, gt_ref, selq_ref, selk_ref, ltri_ref, *rest, aliased):
    if aliased:
        rest = rest[2:]
    kout_ref, logf_ref, qaug_ref, kaug_ref, vt_ref, cend_ref, carry_ref = rest
    t = pl.program_id(1)

    @pl.when(t == 0)
    def _():
        carry_ref[...] = c0_ref[0]

    logf = _log_sigmoid(small_ref[...] + bf_ref[...])
    logf_ref[...] = logf[:, SMALL_F0:SMALL_F0 + N_HEADS_ATT]
    c = _cum_logf(logf, carry_ref, ltri_ref)
    cend_ref[0] = carry_ref[...]
    cp = _c_parts(c * LOG2E, shift_ref[0, 0:1, :])

    qn = _head_rmsnorm(q_ref[...], g_ref, gt_ref, qw_ref[...])
    _head_rows(qn, _dot(cp, selq_ref[...]), qaug_ref, ATT_SCALE * LOG2E, False)
    kn = _head_rmsnorm(k_ref[...], g_ref, gt_ref, kw_ref[...])
    _head_rows(kn, _dot(cp, selk_ref[...]), kaug_ref, 1.0, True)
    kout_ref[...] = kn
    _write_vt(v_ref[...], vt_ref)


def _prep_cache_kernel(k_ref, v_ref, logf_ref, c0_ref, g_ref, selk_ref, ltri_ref,
                       kaug_ref, vt_ref, cend_ref, kmax_ref, carry_ref, *, tail_rows):
    t = pl.program_id(1)
    n_cache_tiles = pl.num_programs(1) - 1

    @pl.when(t == 0)
    def _():
        carry_ref[...] = c0_ref[0]
        kmax_ref[...] = jnp.zeros(kmax_ref.shape, F32)

    @pl.when(t < n_cache_tiles)
    def _():
        c = _cum_logf(logf_ref[0], carry_ref, ltri_ref)
        cend_ref[0] = carry_ref[...]
        k = k_ref[0]
        _head_rows(k, _dot(_c_parts(c * LOG2E), selk_ref[...]), kaug_ref, 1.0, True)
        _write_vt(v_ref[0], vt_ref)
        sq = k * k
        hi = sq.astype(BF16)
        lo = (sq - hi.astype(F32)).astype(BF16)
        ssq = _dot(hi, g_ref[...]) + _dot(lo, g_ref[...])
        kmax_ref[0] = jnp.maximum(kmax_ref[0], jnp.max(ssq, axis=0, keepdims=True))

    @pl.when(t == n_cache_tiles)
    def _():
        kaug_ref[0, :, 0:tail_rows, :] = jnp.zeros((N_HEADS_ATT, tail_rows, LANES), BF16)
        vt_ref[0, :, :, 0:tail_rows] = jnp.zeros((N_HEADS_ATT, VT_ROWS, tail_rows), BF16)


def _aug_selectors():
    h = N_HEADS_ATT
    rows = jnp.arange(LANES)[:, None]
    cols = jnp.arange(LANES)[None, :]
    head = cols // AUG_W
    pos = cols % AUG_W
    one_row = rows == CP_ONE
    selq = jnp.zeros((LANES, LANES), F32)
    selk = jnp.zeros((LANES, LANES), F32)
    for part in range(3):
        part_row = rows == part * h + head
        selq = selq + jnp.where(part_row & (pos == part), 1.0, 0.0)
        selq = selq + jnp.where(one_row & (pos == 3 + part), 1.0, 0.0)
        selk = selk + jnp.where(one_row & (pos == part), 1.0, 0.0)
        selk = selk - jnp.where(part_row & (pos == 3 + part), 1.0, 0.0)
    selq = selq - jnp.where((rows == CP_SHIFT) & (pos == AUG_SHIFT - AUG0), 1.0, 0.0)
    selk = selk + jnp.where(one_row & (pos == AUG_SHIFT - AUG0), 1.0, 0.0)
    return selq.astype(BF16), selk.astype(BF16)


def _head_group_matrices():
    j = jnp.arange(ATT_WIDTH)[:, None] // HEAD_DIM
    h = jnp.arange(LANES)[None, :]
    g = jnp.where(j == h, 1.0, 0.0).astype(BF16)
    return g, g.T


def _ltri(n, block=None):
    r = jnp.arange(n)[:, None]
    c = jnp.arange(n)[None, :]
    keep = c <= r
    if block is not None:
        keep = keep & (r // block == c // block)
    return jnp.where(keep, 1.0, 0.0).astype(BF16)


def _prep_new(qk, v, small, c0, shift, bf128, qw, kw, batch, seq, tk, bufs=None, t_off_blocks=0):
    nt = seq // tk
    h = N_HEADS_ATT
    g, gt = _head_group_matrices()
    selq, selk = _aug_selectors()
    ltri = _ltri(tk)
    rows = batch * seq
    const = lambda shape: pl.BlockSpec(shape, lambda b, t: tuple(0 for _ in shape))
    in_specs = [
        pl.BlockSpec((tk, ATT_WIDTH), lambda b, t: (b * nt + t, 0)),
        pl.BlockSpec((tk, ATT_WIDTH), lambda b, t: (b * nt + t, 1)),
        pl.BlockSpec((tk, ATT_WIDTH), lambda b, t: (b * nt + t, 0)),
        pl.BlockSpec((tk, LANES), lambda b, t: (b * nt + t, 0)),
        pl.BlockSpec((1, SUBLANES, LANES), lambda b, t: (b, 0, 0)),
        pl.BlockSpec((1, SUBLANES, LANES), lambda b, t: (b, 0, 0)),
        const((1, LANES)), const((1, ATT_WIDTH)), const((1, ATT_WIDTH)),
        const((ATT_WIDTH, LANES)), const((LANES, ATT_WIDTH)),
        const((LANES, LANES)), const((LANES, LANES)), const((tk, tk)),
    ]
    args = [qk, qk, v, small, c0, shift, bf128, qw, kw, g, gt, selq, selk, ltri]
    if bufs is None:
        kv_len = seq
        aliases = {}
    else:
        kv_len = bufs[0].shape[2]
        in_specs += [pl.BlockSpec(memory_space=pl.ANY), pl.BlockSpec(memory_space=pl.ANY)]
        aliases = {len(args): 3, len(args) + 1: 4}
        args += list(bufs)
    o = t_off_blocks
    out_specs = [
        pl.BlockSpec((tk, ATT_WIDTH), lambda b, t: (b * nt + t, 0)),
        pl.BlockSpec((tk, h), lambda b, t: (b * nt + t, 0)),
        pl.BlockSpec((1, h, tk, LANES), lambda b, t: (b, 0, t, 0)),
        pl.BlockSpec((1, h, tk, LANES), lambda b, t: (b, 0, t + o, 0)),
        pl.BlockSpec((1, h, VT_ROWS, tk), lambda b, t: (b, 0, 0, t + o)),
        pl.BlockSpec((1, SUBLANES, LANES), lambda b, t: (b, 0, 0)),
    ]
    out_shape = [
        jax.ShapeDtypeStruct((rows, ATT_WIDTH), F32),
        jax.ShapeDtypeStruct((rows, h), F32),
        jax.ShapeDtypeStruct((batch, h, seq, LANES), BF16),
        jax.ShapeDtypeStruct((batch, h, kv_len, LANES), BF16),
        jax.ShapeDtypeStruct((batch, h, VT_ROWS, kv_len), BF16),
        jax.ShapeDtypeStruct((batch, SUBLANES, LANES), F32),
    ]
    return pl.pallas_call(
        functools.partial(_prep_new_kernel, aliased=bufs is not None),
        grid=(batch, nt),
        in_specs=in_specs,
        out_specs=out_specs,
        out_shape=out_shape,
        scratch_shapes=[pltpu.VMEM((SUBLANES, LANES), F32)],
        input_output_aliases=aliases,
        compiler_params=pltpu.CompilerParams(
            dimension_semantics=("arbitrary", "arbitrary"), vmem_limit_bytes=VMEM_LIMIT),
        name="prep_new",
    )(*args)


def _prep_cache(cache_k, cache_v, cache_logf128, c0, tk, kv_len):
    batch, past, _ = cache_k.shape
    nt = past // tk
    h = N_HEADS_ATT
    g, _ = _head_group_matrices()
    _, selk = _aug_selectors()
    ltri = _ltri(tk)
    const = lambda shape: pl.BlockSpec(shape, lambda b, t: tuple(0 for _ in shape))
    assert kv_len - past <= tk
    last = nt - 1
    return pl.pallas_call(
        functools.partial(_prep_cache_kernel, tail_rows=kv_len - past),
        grid=(batch, nt + 1),
        in_specs=[
            pl.BlockSpec((1, tk, ATT_WIDTH), lambda b, t: (b, jnp.minimum(t, last), 0)),
            pl.BlockSpec((1, tk, ATT_WIDTH), lambda b, t: (b, jnp.minimum(t, last), 0)),
            pl.BlockSpec((1, tk, LANES), lambda b, t: (b, jnp.minimum(t, last), 0)),
            pl.BlockSpec((1, SUBLANES, LANES), lambda b, t: (b, 0, 0)),
            const((ATT_WIDTH, LANES)), const((LANES, LANES)), const((tk, tk)),
        ],
        out_specs=[
            pl.BlockSpec((1, h, tk, LANES), lambda b, t: (b, 0, t, 0)),
            pl.BlockSpec((1, h, VT_ROWS, tk), lambda b, t: (b, 0, 0, t)),
            pl.BlockSpec((1, SUBLANES, LANES), lambda b, t: (b, 0, 0)),
            pl.BlockSpec((1, 1, LANES), lambda b, t: (b, 0, 0)),
        ],
        out_shape=[
            jax.ShapeDtypeStruct((batch, h, kv_len, LANES), BF16),
            jax.ShapeDtypeStruct((batch, h, VT_ROWS, kv_len), BF16),
            jax.ShapeDtypeStruct((batch, SUBLANES, LANES), F32),
            jax.ShapeDtypeStruct((batch, 1, LANES), F32),
        ],
        scratch_shapes=[pltpu.VMEM((SUBLANES, LANES), F32)],
        compiler_params=pltpu.CompilerParams(
            dimension_semantics=("arbitrary", "arbitrary"), vmem_limit_bytes=VMEM_LIMIT),
        name="prep_cache",
    )(cache_k, cache_v, cache_logf128, c0, g, selk, ltri)


def _flash_kernel(q_ref, k_ref, vt_ref, o_ref, s0_ref, s1_ref, cm_ref, m_ref, acc_ref,
                  *, tq, tkf, q_off):
    qi = pl.program_id(2)
    q_start = q_off + qi * tq
    n_full = q_start // tkf
    acc_ref[...] = jnp.zeros(acc_ref.shape, F32)
    m_ref[...] = jnp.full(m_ref.shape, NEG_BIG, F32)
    bufs = (s0_ref, s1_ref)

    def scores(k_start, rows, masked, slot):
        for o in range(2):
            kc = k_ref[0, o, pl.ds(k_start, rows), :]
            st = lax.dot_general(kc, q_ref[0, o], (((1,), (1,)), ((), ())),
                                 preferred_element_type=F32)
            if masked:
                st = jnp.where(_row_iota(st.shape) <= _lane_iota(st.shape), st, NEG_BIG)
            bufs[slot][o, 0:rows, :] = st
            cm_ref[slot, o] = jnp.max(st, axis=0, keepdims=True)

    def consume(k_start, rows, slot):
        for o in range(2):
            m_prev = m_ref[o]
            m_new = jnp.maximum(m_prev, cm_ref[slot, o])
            alpha = jnp.exp2(m_prev - m_new)
            p = jnp.exp2(bufs[slot][o, 0:rows, :] - m_new)
            pv = _dot(vt_ref[0, o, :, pl.ds(k_start, rows)], p.astype(BF16))
            acc_ref[o] = alpha * acc_ref[o] + pv
            m_ref[o] = m_new

    full_start = lambda f: pl.multiple_of(f * tkf, tkf)
    diag_start = pl.multiple_of(q_start, tq)
    scores(diag_start, tq, True, 0)

    @pl.when(n_full > 0)
    def _():
        scores(full_start(0), tkf, False, 1)

    consume(diag_start, tq, 0)

    pairs = jnp.maximum(n_full - 1, 0) // 2

    def body(t, carry):
        f = 2 * t
        scores(full_start(f + 1), tkf, False, 0)
        consume(full_start(f), tkf, 1)
        scores(full_start(f + 2), tkf, False, 1)
        consume(full_start(f + 1), tkf, 0)
        return carry

    lax.fori_loop(0, pairs, body, 0)
    f_rem = 2 * pairs
    left = n_full - f_rem

    @pl.when(left == 2)
    def _():
        scores(full_start(f_rem + 1), tkf, False, 0)
        consume(full_start(f_rem), tkf, 1)
        consume(full_start(f_rem + 1), tkf, 0)

    @pl.when(left == 1)
    def _():
        consume(full_start(f_rem), tkf, 1)

    outs = [acc_ref[o, 0:HEAD_DIM, :] / acc_ref[o, HEAD_DIM:HEAD_DIM + 1, :] for o in range(2)]
    o_ref[0] = jnp.concatenate(outs, axis=0).T.astype(BF16)


FLASH_TRIP_KEYS = 3072


def _flash_fixed_kernel(q_ref, k_ref, vt_ref, o_ref, pa_ref, pb_ref, acc_ref, *, tq, tkf, q_off):
    qi = pl.program_id(2)
    q_start = q_off + qi * tq
    n_full = q_start // tkf
    slot_of = lambda f_parity: pa_ref if f_parity else pb_ref

    def probs(k_start, rows, masked, p_ref):
        for o in range(2):
            kc = k_ref[0, o, pl.ds(k_start, rows), :]
            st = lax.dot_general(kc, q_ref[0, o], (((1,), (1,)), ((), ())),
                                 preferred_element_type=F32)
            if masked:
                st = jnp.where(_row_iota(st.shape) <= _lane_iota(st.shape), st, NEG_BIG)
            p_ref[o, 0:rows, :] = jnp.exp2(st).astype(BF16)

    def accumulate(k_start, rows, p_ref, first=False):
        for o in range(2):
            pv = _dot(vt_ref[0, o, :, pl.ds(k_start, rows)], p_ref[o, 0:rows, :])
            acc_ref[o] = pv if first else acc_ref[o] + pv

    full_start = lambda f: pl.multiple_of(f * tkf, tkf)
    diag_start = pl.multiple_of(q_start, tq)
    probs(diag_start, tq, True, pa_ref)

    if tq == tkf:
        start_of = lambda g: pl.multiple_of(jnp.where(g == 0, q_start, (g - 1) * tkf), tkf)
        acc_ref[...] = jnp.zeros(acc_ref.shape, F32)
        n_steps = n_full
        enabled = True

        def step(g, parity):
            probs(full_start(g), tkf, False, slot_of(parity))
            accumulate(start_of(g), tkf, slot_of(1 - parity))

        last = lambda g, parity: accumulate(start_of(g), tkf, slot_of(1 - parity))
    else:
        @pl.when(n_full > 0)
        def _():
            probs(full_start(0), tkf, False, slot_of(0))

        accumulate(diag_start, tq, pa_ref, first=True)
        n_steps = jnp.maximum(n_full - 1, 0)
        enabled = n_full > 0

        def step(f, parity):
            probs(full_start(f + 1), tkf, False, slot_of(1 - parity))
            accumulate(full_start(f), tkf, slot_of(parity))

        last = lambda f, parity: accumulate(full_start(f), tkf, slot_of(parity))

    unroll = max(2, FLASH_TRIP_KEYS // tkf)

    def body(t, carry):
        for i in range(unroll):
            step(unroll * t + i, i % 2)
        return carry

    lax.fori_loop(0, n_steps // unroll, body, 0)
    f0 = (n_steps // unroll) * unroll
    left = n_steps - f0
    for r in range(unroll):
        @pl.when(enabled & (left == r))
        def _(r=r):
            for i in range(r):
                step(f0 + i, i % 2)
            last(f0 + r, r % 2)

    outs = [acc_ref[o, 0:HEAD_DIM, :] / acc_ref[o, HEAD_DIM:HEAD_DIM + 1, :] for o in range(2)]
    o_ref[0] = jnp.concatenate(outs, axis=0).T.astype(BF16)


def _flash(q_aug, k_aug, vt, tq, tkf, q_off, fixed_reference):
    batch, h, tq_total, _ = q_aug.shape
    kv_len = k_aug.shape[2]
    nq = tq_total // tq
    if fixed_reference:
        body = _flash_fixed_kernel
        scratch = [
            pltpu.VMEM((2, max(tq, tkf), tq), BF16),
            pltpu.VMEM((2, max(tq, tkf), tq), BF16),
            pltpu.VMEM((2, VT_ROWS, tq), F32),
        ]
    else:
        body = _flash_kernel
        scratch = [
            pltpu.VMEM((2, max(tq, tkf), tq), F32),
            pltpu.VMEM((2, max(tq, tkf), tq), F32),
            pltpu.VMEM((2, 2, 1, tq), F32),
            pltpu.VMEM((2, 1, tq), F32),
            pltpu.VMEM((2, VT_ROWS, tq), F32),
        ]
    return pl.pallas_call(
        functools.partial(body, tq=tq, tkf=tkf, q_off=q_off),
        grid=(batch, h // 2, nq),
        in_specs=[
            pl.BlockSpec((1, 2, tq, LANES), lambda b, hp, qi: (b, hp, qi, 0)),
            pl.BlockSpec((1, 2, kv_len, LANES), lambda b, hp, qi: (b, hp, 0, 0)),
            pl.BlockSpec((1, 2, VT_ROWS, kv_len), lambda b, hp, qi: (b, hp, 0, 0)),
        ],
        out_specs=pl.BlockSpec((1, tq, LANES), lambda b, hp, qi: (b, qi, hp)),
        out_shape=jax.ShapeDtypeStruct((batch, tq_total, ATT_WIDTH), BF16),
        scratch_shapes=scratch,
        compiler_params=pltpu.CompilerParams(
            dimension_semantics=("arbitrary", "arbitrary", "arbitrary"),
            vmem_limit_bytes=VMEM_LIMIT),
        name="flash_fixed" if fixed_reference else "flash",
    )(q_aug, k_aug, vt)


B_LANE0 = D_INNER
C_LANE0 = D_INNER + N_GROUPS * D_STATE
PK_HI0 = SMALL_DT0
PK_MID0 = PK_HI0 + N_HEADS_SSM
PK_LO0 = PK_MID0 + N_HEADS_SSM


def _pack_head_parts(x):
    hi, mid, lo = _split3(x)
    lane = _lane_iota(x.shape)
    packed = jnp.where(
        lane < PK_MID0, hi.astype(F32),
        jnp.where(lane < PK_LO0, pltpu.roll(mid.astype(F32), N_HEADS_SSM, 1),
                  pltpu.roll(lo.astype(F32), 2 * N_HEADS_SSM, 1)))
    return packed.astype(BF16)


def _head_expand_matrix():
    rows = jnp.arange(LANES)[:, None]
    head = jnp.arange(D_INNER)[None, :] // SSM_HEAD_DIM
    hit = (rows == PK_HI0 + head) | (rows == PK_MID0 + head) | (rows == PK_LO0 + head)
    return jnp.where(hit, 1.0, 0.0).astype(BF16)


def _ssd_kernel(z_ref, xs_ref, bc_ref, small_ref, cpast_ref, h0_ref, cw_ref, cb_ref,
                dtb_ref, alog_ref, dskip_ref, nw_ref, lblk_ref, sel_ref,
                y_ref, cnew_ref, hlast_ref,
                xbuf_ref, ht_ref, ae_ref, dte_ref, ybuf_ref, *, tb, n_valid, tail_row):
    t = pl.program_id(1)
    nt = pl.num_programs(1)
    n_state_blocks = D_INNER // LANES

    n_conv_slabs = CONV_DIM // LANES
    n_inner_slabs = D_INNER // LANES

    @pl.when(t == 0)
    def _():
        for j in range(n_conv_slabs):
            xbuf_ref[j, 0:SUBLANES, :] = cpast_ref[0, :, LANES * j:LANES * (j + 1)]
        for i in range(n_state_blocks):
            ht_ref[:, LANES * i:LANES * (i + 1)] = h0_ref[0, LANES * i:LANES * (i + 1), :].T

    for j in range(n_inner_slabs):
        xbuf_ref[j, SUBLANES:SUBLANES + tb, :] = xs_ref[:, LANES * j:LANES * (j + 1)]
        xbuf_ref[n_inner_slabs + j, SUBLANES:SUBLANES + tb, :] = bc_ref[:, LANES * j:LANES * (j + 1)]

    lane = _lane_iota((tb, LANES))
    row = _row_iota((tb, LANES)) + t * tb
    live = (lane >= SMALL_DT0) & (lane < SMALL_DT0 + N_HEADS_SSM) & (row < n_valid)
    dt = jnp.where(live, _softplus(small_ref[...] + dtb_ref[...]), 0.0)
    a = dt * (-jnp.exp(alog_ref[...]))
    acum = _dot3(lblk_ref[...], a)
    ae_all = _dot(_pack_head_parts(acum), sel_ref[...])
    dte_all = _dot(_pack_head_parts(dt), sel_ref[...])
    for j in range(n_inner_slabs):
        ae_ref[j] = ae_all[:, LANES * j:LANES * (j + 1)]
        dte_ref[j] = dte_all[:, LANES * j:LANES * (j + 1)]

    gw = GROUP_WIDTH
    token_of = lambda r: 32 * (r // 32) + (r // SUBLANES) % 4 + 4 * (r % SUBLANES)
    sub = _row_iota((SSD_CHUNK, gw))
    pos = _lane_iota((SSD_CHUNK, gw)) % SSM_HEAD_DIM
    diag = pos == sub
    causal = token_of(pos) <= token_of(sub)
    blk_row = _row_iota((gw, gw)) // SSM_HEAD_DIM
    blk_col = _lane_iota((gw, gw)) // SSM_HEAD_DIM
    same_head = blk_row == blk_col
    windows = [(32 * half + a) for half in range(2) for a in range(4)]

    def load_perm(ref, j, row0):
        return jnp.concatenate(
            [ref[j, pl.ds(row0 + w, SUBLANES, stride=4), :] for w in windows], axis=0)

    def conv_slab(c0, j):
        cols = slice(LANES * j, LANES * (j + 1))
        out = cb_ref[:, cols]
        for i in range(CONV_W):
            out = out + load_perm(xbuf_ref, j, c0 + SUBLANES - (CONV_W - 1) + i) * cw_ref[i:i + 1, cols]
        return _silu(out)

    def chunk(c, carry):
        c0 = pl.multiple_of(c * SSD_CHUNK, SSD_CHUNK)
        rows = pl.ds(c0, SSD_CHUNK)
        for g in range(N_GROUPS):
            lanes = slice(gw * g, gw * (g + 1))
            j0 = 2 * g
            xs = jnp.concatenate([conv_slab(c0, j0), conv_slab(c0, j0 + 1)], axis=1)
            bm = conv_slab(c0, n_inner_slabs + g)
            cm = conv_slab(c0, n_inner_slabs + N_GROUPS + g)
            ae = jnp.concatenate([load_perm(ae_ref, j0, c0), load_perm(ae_ref, j0 + 1, c0)], axis=1)
            dte = jnp.concatenate([load_perm(dte_ref, j0, c0), load_perm(dte_ref, j0 + 1, c0)], axis=1)
            xdt = xs * dte
            bm16 = bm.astype(BF16)
            cm16 = cm.astype(BF16)

            b4 = jnp.concatenate([bm16] * HEADS_PER_GROUP, axis=0)
            cb4 = lax.dot_general(cm16, b4, (((1,), (1,)), ((), ())), preferred_element_type=F32)
            a_row = jnp.sum(jnp.where(diag, ae, 0.0), axis=0, keepdims=True)
            decay = jnp.exp(jnp.where(causal, ae - a_row, NEG_BIG))
            xdt16 = xdt.astype(BF16)
            x4 = jnp.where(same_head, jnp.concatenate([xdt16] * HEADS_PER_GROUP, axis=0),
                           jnp.zeros((), BF16))
            y = _dot((cb4 * decay).astype(BF16), x4)

            hprev = ht_ref[:, lanes]
            y = y + jnp.exp(ae) * _dot(cm16, hprev.astype(BF16))
            a_last = ae[SSD_CHUNK - 1:SSD_CHUNK, :]
            xw = (xdt * jnp.exp(a_last - ae)).astype(BF16)
            upd = lax.dot_general(bm16, xw, (((0,), (0,)), ((), ())), preferred_element_type=F32)
            ht_ref[:, lanes] = hprev * jnp.exp(a_last) + upd

            y = y + xs * dskip_ref[:, lanes]
            for jj in range(2):
                for v, w in enumerate(windows):
                    ybuf_ref[jj, pl.ds(c0 + w, SUBLANES, stride=4), :] = (
                        y[SUBLANES * v:SUBLANES * (v + 1), LANES * jj:LANES * (jj + 1)])
            y = jnp.concatenate([ybuf_ref[0, rows, :], ybuf_ref[1, rows, :]], axis=1)
            y = y * _silu(z_ref[rows, lanes])
            ms = jnp.mean(y * y, axis=-1, keepdims=True)
            y_ref[rows, lanes] = ((y * lax.rsqrt(ms + EPS)) * nw_ref[:, lanes]).astype(BF16)
        return carry

    lax.fori_loop(0, tb // SSD_CHUNK, chunk, 0)

    for j in range(n_conv_slabs):
        cnew_ref[0, :, LANES * j:LANES * (j + 1)] = xbuf_ref[j, tail_row:tail_row + SUBLANES, :]
        xbuf_ref[j, 0:SUBLANES, :] = xbuf_ref[j, tb:tb + SUBLANES, :]

    @pl.when(t == nt - 1)
    def _():
        for i in range(n_state_blocks):
            hlast_ref[0, LANES * i:LANES * (i + 1), :] = ht_ref[:, LANES * i:LANES * (i + 1)].T


def _ssd(zxg, small, conv_past8, h0, cw, cb, dtb128, alog128, dskip_e, nw, batch, seq, tb, n_valid):
    nt = seq // tb
    rows = batch * seq
    tail_row = n_valid - (seq - tb)
    lblk = _ltri(tb, SSD_CHUNK)
    sel = _head_expand_matrix()
    const = lambda shape: pl.BlockSpec(shape, lambda b, t: tuple(0 for _ in shape))
    return pl.pallas_call(
        functools.partial(_ssd_kernel, tb=tb, n_valid=n_valid, tail_row=tail_row),
        grid=(batch, nt),
        in_specs=[
            pl.BlockSpec((tb, D_INNER), lambda b, t: (b * nt + t, 0)),
            pl.BlockSpec((tb, D_INNER), lambda b, t: (b * nt + t, 1)),
            pl.BlockSpec((tb, D_INNER), lambda b, t: (b * nt + t, 2)),
            pl.BlockSpec((tb, LANES), lambda b, t: (b * nt + t, 0)),
            pl.BlockSpec((1, SUBLANES, CONV_DIM), lambda b, t: (b, 0, 0)),
            pl.BlockSpec((1, D_INNER, D_STATE), lambda b, t: (b, 0, 0)),
            const((CONV_W, CONV_DIM)), const((1, CONV_DIM)),
            const((1, LANES)), const((1, LANES)), const((1, D_INNER)), const((1, D_INNER)),
            const((tb, tb)), const((LANES, D_INNER)),
        ],
        out_specs=[
            pl.BlockSpec((tb, D_INNER), lambda b, t: (b * nt + t, 0)),
            pl.BlockSpec((1, SUBLANES, CONV_DIM), lambda b, t: (b, 0, 0)),
            pl.BlockSpec((1, D_INNER, D_STATE), lambda b, t: (b, 0, 0)),
        ],
        out_shape=[
            jax.ShapeDtypeStruct((rows, D_INNER), BF16),
            jax.ShapeDtypeStruct((batch, SUBLANES, CONV_DIM), F32),
            jax.ShapeDtypeStruct((batch, D_INNER, D_STATE), F32),
        ],
        scratch_shapes=[
            pltpu.VMEM((CONV_DIM // LANES, tb + SUBLANES, LANES), F32),
            pltpu.VMEM((D_STATE, D_INNER), F32),
            pltpu.VMEM((D_INNER // LANES, tb, LANES), F32),
            pltpu.VMEM((D_INNER // LANES, tb, LANES), F32),
            pltpu.VMEM((2, tb, LANES), F32),
        ],
        compiler_params=pltpu.CompilerParams(
            dimension_semantics=("arbitrary", "arbitrary"), vmem_limit_bytes=VMEM_LIMIT),
        name="ssd",
    )(zxg, zxg, zxg, small, conv_past8, h0, cw, cb, dtb128, alog128, dskip_e, nw, lblk, sel)


FF_CHUNK = 1024


def _merge_kernel(att_ref, ys_ref, ga_ref, gs_ref, x_ref, wpa_ref, wpb_ref, wout_ref,
                  n2_ref, wup_ref, wdn_ref, o_ref):
    merged = (_sigmoid(ga_ref[...]) * _dot(att_ref[...], wpa_ref[...])
              + _sigmoid(gs_ref[...]) * _dot(ys_ref[...], wpb_ref[...]))
    x1 = x_ref[...] + _dot(merged.astype(BF16), wout_ref[...])
    ms = jnp.mean(x1 * x1, axis=-1, keepdims=True)
    hn = ((x1 * lax.rsqrt(ms + EPS)) * n2_ref[...]).astype(BF16)
    acc = x1
    for c in range(D_FF // FF_CHUNK):
        cols = slice(FF_CHUNK * c, FF_CHUNK * (c + 1))
        up = jnp.maximum(_dot(hn, wup_ref[:, cols]), 0.0)
        acc = acc + _dot((up * up).astype(BF16), wdn_ref[cols, :])
    o_ref[...] = acc


def _merge(att, ys, zxg, x2d, wpa, wpb, wout, n2, wup, wdn, tm):
    rows = x2d.shape[0]
    ga_blk = 6
    gs_blk = 7
    resident = lambda shape: pl.BlockSpec(shape, lambda i: tuple(0 for _ in shape),
                                          pipeline_mode=pl.Buffered(1))
    return pl.pallas_call(
        _merge_kernel,
        grid=(rows // tm,),
        in_specs=[
            pl.BlockSpec((tm, ATT_WIDTH), lambda i: (i, 0)),
            pl.BlockSpec((tm, D_INNER), lambda i: (i, 0)),
            pl.BlockSpec((tm, D_MODEL), lambda i: (i, ga_blk)),
            pl.BlockSpec((tm, D_MODEL), lambda i: (i, gs_blk)),
            pl.BlockSpec((tm, D_MODEL), lambda i: (i, 0)),
            resident((ATT_WIDTH, D_MODEL)), resident((D_INNER, D_MODEL)),
            resident((D_MODEL, D_MODEL)), resident((1, D_MODEL)),
            resident((D_MODEL, D_FF)), resident((D_FF, D_MODEL)),
        ],
        out_specs=pl.BlockSpec((tm, D_MODEL), lambda i: (i, 0)),
        out_shape=jax.ShapeDtypeStruct((rows, D_MODEL), F32),
        compiler_params=pltpu.CompilerParams(
            dimension_semantics=("arbitrary",), vmem_limit_bytes=VMEM_LIMIT),
        name="merge",
    )(att, ys, zxg, zxg, x2d, wpa, wpb, wout, n2, wup, wdn)


def _pick_tile(n, prefs):
    for p in prefs:
        if n % p == 0:
            return p
    raise ValueError(f"no tile in {prefs} divides {n}")


def _pad_lanes(vec, lane0):
    return jnp.zeros((1, LANES), F32).at[0, lane0:lane0 + vec.shape[0]].set(vec.astype(F32))


def _layer(x, past, p, n_valid):
    batch, seq, _ = x.shape
    rows = batch * seq
    x2d = x.reshape(rows, D_MODEL)

    tm = _pick_tile(rows, (2048, 1024, 512, 256, 128))
    qk, v, zxg, small = _inproj(x2d, p["norm1_w"], p["w_main"], p["w_small"], tm)

    tk = _pick_tile(seq, (512, 256, 128)) if past is None else LANES
    zeros_c = jnp.zeros((batch, SUBLANES, LANES), F32)
    k2_new = HEAD_DIM * jnp.max(jnp.square(p["kw"]))
    if past is None:
        k2 = jnp.full((batch,), k2_new, F32)
    else:
        cache_k, cache_v, cache_logf, conv_past, h0 = past
        n_past = cache_k.shape[1]
        kv_len = n_past + seq
        logf128 = jnp.pad(cache_logf.astype(F32), ((0, 0), (0, 0), (0, LANES - N_HEADS_ATT)))
        tkf = _pick_tile(n_past, (1024, 512, 256, 128))
        k_buf, vt_buf, c_past, k2_cache = _prep_cache(cache_k, cache_v, logf128, zeros_c, tkf, kv_len)
        k2 = jnp.maximum(k2_new, jnp.max(k2_cache, axis=(1, 2)))
    q_norm = jnp.sqrt(HEAD_DIM * jnp.max(jnp.square(p["qw"])))
    shift = (SHIFT_MARGIN * ATT_SCALE * LOG2E) * q_norm * jnp.sqrt(k2)
    shift = shift.astype(BF16).astype(F32)
    shift_rows = jnp.broadcast_to(shift[:, None, None], (batch, SUBLANES, LANES))
    if past is None:
        k_out, logf, q_aug, k_aug, vt, _ = _prep_new(
            qk, v, small, zeros_c, shift_rows, p["bf128"], p["qw"], p["kw"], batch, seq, tk)
        tkf, q_off = tk, 0
        conv_past8 = jnp.zeros((batch, SUBLANES, CONV_DIM), F32)
        h0 = jnp.zeros((batch, D_INNER, D_STATE), F32)
    else:
        k_out, logf, q_aug, k_aug, vt, _ = _prep_new(
            qk, v, small, c_past, shift_rows, p["bf128"], p["qw"], p["kw"], batch, seq, tk,
            bufs=(k_buf, vt_buf), t_off_blocks=n_past // tk)
        q_off = n_past
        conv_past8 = jnp.pad(conv_past.astype(F32), ((0, 0), (SUBLANES - (CONV_W - 1), 0), (0, 0)))
        h0 = h0.astype(F32).reshape(batch, D_INNER, D_STATE)

    att = lax.cond(jnp.max(shift) <= SHIFT_MAX,
                   lambda: _flash(q_aug, k_aug, vt, tk, tkf, q_off, True),
                   lambda: _flash(q_aug, k_aug, vt, tk, tkf, q_off, False))

    tb = _pick_tile(seq, (256, 128))
    ys, conv_new8, h_last = _ssd(zxg, small, conv_past8, h0, p["conv_w"], p["conv_b"],
                                 p["dtb128"], p["alog128"], p["dskip_e"], p["ssm_norm_w"],
                                 batch, seq, tb, n_valid)

    tmm = _pick_tile(rows, (512, 256, 128))
    y = _merge(att.reshape(rows, ATT_WIDTH), ys, zxg, x2d, p["w_pa"], p["w_pb"], p["w_out"],
               p["norm2_w"], p["w_up"], p["w_down"], tmm)

    nv = n_valid
    y = y.reshape(batch, seq, D_MODEL)[:, :nv]
    k_new = k_out.reshape(batch, seq, N_HEADS_ATT, HEAD_DIM)[:, :nv]
    v_new = v.reshape(batch, seq, N_HEADS_ATT, HEAD_DIM)[:, :nv]
    logf_new = logf.reshape(batch, seq, N_HEADS_ATT)[:, :nv]
    conv_new = conv_new8[:, SUBLANES - (CONV_W - 1):, :]
    h_new = h_last.reshape(batch, N_HEADS_SSM, SSM_HEAD_DIM, D_STATE)
    return y, (k_new, v_new, logf_new, conv_new, h_new)


def _layer_params(l, norm1_w, w_in, b_f, q_norm_w, k_norm_w, conv_w, conv_b, dt_bias, A_log,
                  D_skip, ssm_norm_w, w_pa, w_pb, w_out, norm2_w, w_up, w_down):
    w = w_in[l]
    o = 0
    seg = {}
    for name, size in (("q", ATT_WIDTH), ("k", ATT_WIDTH), ("v", ATT_WIDTH), ("f", N_HEADS_ATT),
                       ("z", D_INNER), ("xbc", CONV_DIM), ("dt", N_HEADS_SSM),
                       ("ga", D_MODEL), ("gs", D_MODEL)):
        seg[name] = w[:, o:o + size]
        o += size
    w_main = jnp.concatenate([seg[n] for n in ("q", "k", "v", "z", "xbc", "ga", "gs")], axis=1)
    w_small = jnp.zeros((D_MODEL, LANES), w.dtype)
    w_small = w_small.at[:, SMALL_F0:SMALL_F0 + N_HEADS_ATT].set(seg["f"])
    w_small = w_small.at[:, SMALL_DT0:SMALL_DT0 + N_HEADS_SSM].set(seg["dt"])
    row = lambda a: a.astype(F32).reshape(1, -1)
    return {
        "norm1_w": row(norm1_w[l]),
        "w_main": w_main.astype(BF16),
        "w_small": w_small.astype(BF16),
        "bf128": _pad_lanes(b_f[l], SMALL_F0),
        "qw": row(jnp.tile(q_norm_w[l], N_HEADS_ATT)),
        "kw": row(jnp.tile(k_norm_w[l], N_HEADS_ATT)),
        "conv_w": conv_w[l].astype(F32),
        "conv_b": row(conv_b[l]),
        "dtb128": _pad_lanes(dt_bias[l], SMALL_DT0),
        "alog128": _pad_lanes(A_log[l], SMALL_DT0),
        "dskip_e": row(jnp.repeat(D_skip[l], SSM_HEAD_DIM)),
        "ssm_norm_w": row(ssm_norm_w[l]),
        "w_pa": w_pa[l].astype(BF16),
        "w_pb": w_pb[l].astype(BF16),
        "w_out": w_out[l].astype(BF16),
        "norm2_w": row(norm2_w[l]),
        "w_up": w_up[l].astype(BF16),
        "w_down": w_down[l].astype(BF16),
    }


def kernel(x_prompt, x_sample, cache_k, cache_v, cache_logf, state_conv, state_ssm, norm1_w, w_in,
           b_f, q_norm_w, k_norm_w, conv_w, conv_b, dt_bias, A_log, D_skip, ssm_norm_w, w_pa, w_pb,
           w_out, norm2_w, w_up, w_down):
    depth = w_in.shape[0]
    dec_batch, dec_seq, _ = x_sample.shape
    dec_pad = -(-dec_seq // LANES) * LANES
    yp = x_prompt
    ys = jnp.pad(x_sample, ((0, 0), (0, dec_pad - dec_seq), (0, 0)))
    states_p, states_s = [], []
    for l in range(depth):
        p = _layer_params(l, norm1_w, w_in, b_f, q_norm_w, k_norm_w, conv_w, conv_b, dt_bias,
                          A_log, D_skip, ssm_norm_w, w_pa, w_pb, w_out, norm2_w, w_up, w_down)
        yp, st_p = _layer(yp, None, p, yp.shape[1])
        n_past = cache_k.shape[2]
        past = (cache_k[l].reshape(dec_batch, n_past, ATT_WIDTH),
                cache_v[l].reshape(dec_batch, n_past, ATT_WIDTH),
                cache_logf[l], state_conv[l], state_ssm[l])
        ys_l, st_s = _layer(ys, past, p, dec_seq)
        states_p.append(st_p)
        states_s.append(st_s)
        if l + 1 < depth:
            ys = jnp.pad(ys_l, ((0, 0), (0, dec_pad - dec_seq), (0, 0)))
        else:
            ys = ys_l
    stack = lambda states, i: jnp.stack([s[i] for s in states], axis=0)
    return (yp, ys,
            stack(states_p, 0), stack(states_p, 1), stack(states_p, 2), stack(states_p, 3),
            stack(states_p, 4),
            stack(states_s, 0), stack(states_s, 1), stack(states_s, 2), stack(states_s, 3),
            stack(states_s, 4))
```

```python
import functools

import jax
import jax.numpy as jnp
from jax import lax
from jax.experimental import pallas as pl
from jax.experimental.pallas import tpu as pltpu

F32 = jnp.float32
BF16 = jnp.bfloat16

D_MODEL = 1024
N_HEADS_ATT = 16
HEAD_DIM = 64
ATT_WIDTH = N_HEADS_ATT * HEAD_DIM
ATT_SCALE = HEAD_DIM ** -0.5
D_INNER = 2048
SSM_HEAD_DIM = 64
N_HEADS_SSM = D_INNER // SSM_HEAD_DIM
N_GROUPS = 8
HEADS_PER_GROUP = N_HEADS_SSM // N_GROUPS
GROUP_WIDTH = D_INNER // N_GROUPS
D_STATE = 128
CONV_W = 4
CONV_DIM = D_INNER + 2 * N_GROUPS * D_STATE
D_FF = 4 * D_MODEL
EPS = 1e-6
SSD_CHUNK = 64

LANES = 128
SUBLANES = 8
NEG_BIG = -1e30
LOG2E = 1.4426950408889634
VMEM_LIMIT = 56 * 1024 * 1024

SMALL_F0 = 0
SMALL_DT0 = N_HEADS_ATT


def _split3(x):
    hi = x.astype(BF16)
    r1 = x - hi.astype(F32)
    mid = r1.astype(BF16)
    lo = (r1 - mid.astype(F32)).astype(BF16)
    return hi, mid, lo


def _dot(a, b):
    return jnp.dot(a, b, preferred_element_type=F32)


def _dot3(lhs_bf16, x):
    hi, mid, lo = _split3(x)
    return _dot(lhs_bf16, hi) + _dot(lhs_bf16, mid) + _dot(lhs_bf16, lo)


def _sigmoid(x):
    return 0.5 * jnp.tanh(0.5 * x) + 0.5


def _silu(x):
    h = 0.5 * x
    return h + h * jnp.tanh(h)


def _softplus(x):
    return jnp.maximum(x, 0.0) + jnp.log1p(jnp.exp(-jnp.abs(x)))


def _log_sigmoid(x):
    return jnp.minimum(x, 0.0) - jnp.log1p(jnp.exp(-jnp.abs(x)))


def _lane_iota(shape):
    return lax.broadcasted_iota(jnp.int32, shape, len(shape) - 1)


def _row_iota(shape):
    return lax.broadcasted_iota(jnp.int32, shape, len(shape) - 2)


IN_TN = 512
IN_QK_TILES = 2 * ATT_WIDTH // IN_TN
IN_V_TILES = ATT_WIDTH // IN_TN
IN_ZXG_TILE0 = IN_QK_TILES + IN_V_TILES
IN_ZXG_TILES = (D_INNER + CONV_DIM + 2 * D_MODEL) // IN_TN
IN_MAIN_TILES = IN_ZXG_TILE0 + IN_ZXG_TILES


def _inproj_kernel(x_ref, nw_ref, wm_ref, ws_ref, qk_ref, v_ref, zxg_ref, small_ref, xn_ref):
    j = pl.program_id(1)

    @pl.when(j == 0)
    def _():
        x = x_ref[...]
        ms = jnp.mean(x * x, axis=-1, keepdims=True)
        xn = (x * lax.rsqrt(ms + EPS)) * nw_ref[...]
        xn_ref[...] = xn.astype(BF16)
        small_ref[...] = _dot(xn_ref[...], ws_ref[...])

    @pl.when(j < IN_QK_TILES)
    def _():
        qk_ref[...] = _dot(xn_ref[...], wm_ref[...])

    @pl.when((j >= IN_QK_TILES) & (j < IN_ZXG_TILE0))
    def _():
        v_ref[...] = _dot(xn_ref[...], wm_ref[...])

    @pl.when(j >= IN_ZXG_TILE0)
    def _():
        zxg_ref[...] = _dot(xn_ref[...], wm_ref[...])


def _inproj(x2d, nw, w_main, w_small, tm):
    rows = x2d.shape[0]
    grid = (rows // tm, IN_MAIN_TILES)
    n_zxg = IN_ZXG_TILES
    return pl.pallas_call(
        _inproj_kernel,
        grid=grid,
        in_specs=[
            pl.BlockSpec((tm, D_MODEL), lambda i, j: (i, 0)),
            pl.BlockSpec((1, D_MODEL), lambda i, j: (0, 0)),
            pl.BlockSpec((D_MODEL, IN_TN), lambda i, j: (0, j)),
            pl.BlockSpec((D_MODEL, LANES), lambda i, j: (0, 0)),
        ],
        out_specs=[
            pl.BlockSpec((tm, IN_TN), lambda i, j: (i, jnp.minimum(j, IN_QK_TILES - 1))),
            pl.BlockSpec((tm, IN_TN), lambda i, j: (i, jnp.clip(j - IN_QK_TILES, 0, IN_V_TILES - 1))),
            pl.BlockSpec((tm, IN_TN), lambda i, j: (i, jnp.clip(j - IN_ZXG_TILE0, 0, n_zxg - 1))),
            pl.BlockSpec((tm, LANES), lambda i, j: (i, 0)),
        ],
        out_shape=[
            jax.ShapeDtypeStruct((rows, IN_QK_TILES * IN_TN), F32),
            jax.ShapeDtypeStruct((rows, IN_V_TILES * IN_TN), F32),
            jax.ShapeDtypeStruct((rows, n_zxg * IN_TN), F32),
            jax.ShapeDtypeStruct((rows, LANES), F32),
        ],
        scratch_shapes=[pltpu.VMEM((tm, D_MODEL), BF16)],
        compiler_params=pltpu.CompilerParams(
            dimension_semantics=("arbitrary", "arbitrary"), vmem_limit_bytes=VMEM_LIMIT),
        name="inproj",
    )(x2d, nw, w_main, w_small)


AUG0 = HEAD_DIM
AUG_SHIFT = AUG0 + 6
AUG_W = LANES // N_HEADS_ATT
CP_ONE = 3 * N_HEADS_ATT
CP_SHIFT = CP_ONE + 1
SHIFT_MAX = 45.0
SHIFT_MARGIN = 1.03
VT_ROWS = HEAD_DIM + 16


def _cum_logf(logf128, carry_ref, ltri_ref):
    tp = logf128.shape[0]
    c = carry_ref[0:1, :] + _dot3(ltri_ref[...], logf128)
    carry_ref[...] = jnp.broadcast_to(c[tp - 1:tp, :], carry_ref.shape)
    return c


def _c_parts(c, shift=None):
    hi, mid, lo = _split3(c)
    lane = _lane_iota(c.shape)
    h = N_HEADS_ATT
    tail = jnp.where(lane == CP_ONE, 1.0, 0.0)
    if shift is not None:
        tail = jnp.where(lane == CP_SHIFT, shift, tail)
    packed = jnp.where(
        lane < h, hi.astype(F32),
        jnp.where(lane < 2 * h, pltpu.roll(mid.astype(F32), h, 1),
                  jnp.where(lane < 3 * h, pltpu.roll(lo.astype(F32), 2 * h, 1), tail)))
    return packed.astype(BF16)


def _head_rows(x, extras, out_ref, scale, zero_tail):
    lane = _lane_iota((x.shape[0], LANES))
    for hp in range(N_HEADS_ATT // 2):
        pair = x[:, LANES * hp:LANES * (hp + 1)] * scale
        for o in range(2):
            h = 2 * hp + o
            xh = pair if o == 0 else pltpu.roll(pair, HEAD_DIM, 1)
            ex = pltpu.roll(extras, (AUG0 - AUG_W * h) % LANES, 1)
            if zero_tail:
                ex = jnp.where(lane < AUG0 + AUG_W, ex, 0.0)
            out_ref[0, h] = jnp.where(lane < HEAD_DIM, xh, ex).astype(BF16)


def _write_vt(v, vt_ref):
    vt = v.T
    tail_shape = (VT_ROWS - HEAD_DIM, v.shape[0])
    tail = jnp.where(_row_iota(tail_shape) == 0, 1.0, 0.0).astype(BF16)
    for h in range(N_HEADS_ATT):
        vt_ref[0, h, 0:HEAD_DIM, :] = vt[HEAD_DIM * h:HEAD_DIM * (h + 1), :].astype(BF16)
        vt_ref[0, h, HEAD_DIM:VT_ROWS, :] = tail


def _head_rmsnorm(x, g_ref, gt_ref, w):
    sq = x * x
    hi = sq.astype(BF16)
    lo = (sq - hi.astype(F32)).astype(BF16)
    ssq = _dot(hi, g_ref[...]) + _dot(lo, g_ref[...])
    rs = lax.rsqrt(ssq * (1.0 / HEAD_DIM) + EPS)
    rs_hi = rs.astype(BF16)
    rs_lo = (rs - rs_hi.astype(F32)).astype(BF16)
    rs_e = _dot(rs_hi, gt_ref[...]) + _dot(rs_lo, gt_ref[...])
    return (x * rs_e) * w


def _prep_new_kernel(q_ref, k_ref, v_ref, small_ref, c0_ref, shift_ref, bf_ref, qw_ref, kw_ref,
                     g_ref, gt_ref, selq_ref, selk_ref, ltri_ref, *rest, aliased):
    if aliased:
        rest = rest[2:]
    kout_ref, logf_ref, qaug_ref, kaug_ref, vt_ref, cend_ref, carry_ref = rest
    t = pl.program_id(1)

    @pl.when(t == 0)
    def _():
        carry_ref[...] = c0_ref[0]

    logf = _log_sigmoid(small_ref[...] + bf_ref[...])
    logf_ref[...] = logf[:, SMALL_F0:SMALL_F0 + N_HEADS_ATT]
    c = _cum_logf(logf, carry_ref, ltri_ref)
    cend_ref[0] = carry_ref[...]
    cp = _c_parts(c * LOG2E, shift_ref[0, 0:1, :])

    qn = _head_rmsnorm(q_ref[...], g_ref, gt_ref, qw_ref[...])
    _head_rows(qn, _dot(cp, selq_ref[...]), qaug_ref, ATT_SCALE * LOG2E, False)
    kn = _head_rmsnorm(k_ref[...], g_ref, gt_ref, kw_ref[...])
    _head_rows(kn, _dot(cp, selk_ref[...]), kaug_ref, 1.0, True)
    kout_ref[...] = kn
    _write_vt(v_ref[...], vt_ref)


def _prep_cache_kernel(k_ref, v_ref, logf_ref, c0_ref, g_ref, selk_ref, ltri_ref,
                       kaug_ref, vt_ref, cend_ref, kmax_ref, carry_ref, *, tail_rows):
    t = pl.program_id(1)
    n_cache_tiles = pl.num_programs(1) - 1

    @pl.when(t == 0)
    def _():
        carry_ref[...] = c0_ref[0]
        kmax_ref[...] = jnp.zeros(kmax_ref.shape, F32)

    @pl.when(t < n_cache_tiles)
    def _():
        c = _cum_logf(logf_ref[0], carry_ref, ltri_ref)
        cend_ref[0] = carry_ref[...]
        k = k_ref[0]
        _head_rows(k, _dot(_c_parts(c * LOG2E), selk_ref[...]), kaug_ref, 1.0, True)
        _write_vt(v_ref[0], vt_ref)
        sq = k * k
        hi = sq.astype(BF16)
        lo = (sq - hi.astype(F32)).astype(BF16)
        ssq = _dot(hi, g_ref[...]) + _dot(lo, g_ref[...])
        kmax_ref[0] = jnp.maximum(kmax_ref[0], jnp.max(ssq, axis=0, keepdims=True))

    @pl.when(t == n_cache_tiles)
    def _():
        kaug_ref[0, :, 0:tail_rows, :] = jnp.zeros((N_HEADS_ATT, tail_rows, LANES), BF16)
        vt_ref[0, :, :, 0:tail_rows] = jnp.zeros((N_HEADS_ATT, VT_ROWS, tail_rows), BF16)


def _aug_selectors():
    h = N_HEADS_ATT
    rows = jnp.arange(LANES)[:, None]
    cols = jnp.arange(LANES)[None, :]
    head = cols // AUG_W
    pos = cols % AUG_W
    one_row = rows == CP_ONE
    selq = jnp.zeros((LANES, LANES), F32)
    selk = jnp.zeros((LANES, LANES), F32)
    for part in range(3):
        part_row = rows == part * h + head
        selq = selq + jnp.where(part_row & (pos == part), 1.0, 0.0)
        selq = selq + jnp.where(one_row & (pos == 3 + part), 1.0, 0.0)
        selk = selk + jnp.where(one_row & (pos == part), 1.0, 0.0)
        selk = selk - jnp.where(part_row & (pos == 3 + part), 1.0, 0.0)
    selq = selq - jnp.where((rows == CP_SHIFT) & (pos == AUG_SHIFT - AUG0), 1.0, 0.0)
    selk = selk + jnp.where(one_row & (pos == AUG_SHIFT - AUG0), 1.0, 0.0)
    return selq.astype(BF16), selk.astype(BF16)


def _head_group_matrices():
    j = jnp.arange(ATT_WIDTH)[:, None] // HEAD_DIM
    h = jnp.arange(LANES)[None, :]
    g = jnp.where(j == h, 1.0, 0.0).astype(BF16)
    return g, g.T


def _ltri(n, block=None):
    r = jnp.arange(n)[:, None]
    c = jnp.arange(n)[None, :]
    keep = c <= r
    if block is not None:
        keep = keep & (r // block == c // block)
    return jnp.where(keep, 1.0, 0.0).astype(BF16)


def _prep_new(qk, v, small, c0, shift, bf128, qw, kw, batch, seq, tk, bufs=None, t_off_blocks=0):
    nt = seq // tk
    h = N_HEADS_ATT
    g, gt = _head_group_matrices()
    selq, selk = _aug_selectors()
    ltri = _ltri(tk)
    rows = batch * seq
    const = lambda shape: pl.BlockSpec(shape, lambda b, t: tuple(0 for _ in shape))
    in_specs = [
        pl.BlockSpec((tk, ATT_WIDTH), lambda b, t: (b * nt + t, 0)),
        pl.BlockSpec((tk, ATT_WIDTH), lambda b, t: (b * nt + t, 1)),
        pl.BlockSpec((tk, ATT_WIDTH), lambda b, t: (b * nt + t, 0)),
        pl.BlockSpec((tk, LANES), lambda b, t: (b * nt + t, 0)),
        pl.BlockSpec((1, SUBLANES, LANES), lambda b, t: (b, 0, 0)),
        pl.BlockSpec((1, SUBLANES, LANES), lambda b, t: (b, 0, 0)),
        const((1, LANES)), const((1, ATT_WIDTH)), const((1, ATT_WIDTH)),
        const((ATT_WIDTH, LANES)), const((LANES, ATT_WIDTH)),
        const((LANES, LANES)), const((LANES, LANES)), const((tk, tk)),
    ]
    args = [qk, qk, v, small, c0, shift, bf128, qw, kw, g, gt, selq, selk, ltri]
    if bufs is None:
        kv_len = seq
        aliases = {}
    else:
        kv_len = bufs[0].shape[2]
        in_specs += [pl.BlockSpec(memory_space=pl.ANY), pl.BlockSpec(memory_space=pl.ANY)]
        aliases = {len(args): 3, len(args) + 1: 4}
        args += list(bufs)
    o = t_off_blocks
    out_specs = [
        pl.BlockSpec((tk, ATT_WIDTH), lambda b, t: (b * nt + t, 0)),
        pl.BlockSpec((tk, h), lambda b, t: (b * nt + t, 0)),
        pl.BlockSpec((1, h, tk, LANES), lambda b, t: (b, 0, t, 0)),
        pl.BlockSpec((1, h, tk, LANES), lambda b, t: (b, 0, t + o, 0)),
        pl.BlockSpec((1, h, VT_ROWS, tk), lambda b, t: (b, 0, 0, t + o)),
        pl.BlockSpec((1, SUBLANES, LANES), lambda b, t: (b, 0, 0)),
    ]
    out_shape = [
        jax.ShapeDtypeStruct((rows, ATT_WIDTH), F32),
        jax.ShapeDtypeStruct((rows, h), F32),
        jax.ShapeDtypeStruct((batch, h, seq, LANES), BF16),
        jax.ShapeDtypeStruct((batch, h, kv_len, LANES), BF16),
        jax.ShapeDtypeStruct((batch, h, VT_ROWS, kv_len), BF16),
        jax.ShapeDtypeStruct((batch, SUBLANES, LANES), F32),
    ]
    return pl.pallas_call(
        functools.partial(_prep_new_kernel, aliased=bufs is not None),
        grid=(batch, nt),
        in_specs=in_specs,
        out_specs=out_specs,
        out_shape=out_shape,
        scratch_shapes=[pltpu.VMEM((SUBLANES, LANES), F32)],
        input_output_aliases=aliases,
        compiler_params=pltpu.CompilerParams(
            dimension_semantics=("arbitrary", "arbitrary"), vmem_limit_bytes=VMEM_LIMIT),
        name="prep_new",
    )(*args)


def _prep_cache(cache_k, cache_v, cache_logf128, c0, tk, kv_len):
    batch, past, _ = cache_k.shape
    nt = past // tk
    h = N_HEADS_ATT
    g, _ = _head_group_matrices()
    _, selk = _aug_selectors()
    ltri = _ltri(tk)
    const = lambda shape: pl.BlockSpec(shape, lambda b, t: tuple(0 for _ in shape))
    assert kv_len - past <= tk
    last = nt - 1
    return pl.pallas_call(
        functools.partial(_prep_cache_kernel, tail_rows=kv_len - past),
        grid=(batch, nt + 1),
        in_specs=[
            pl.BlockSpec((1, tk, ATT_WIDTH), lambda b, t: (b, jnp.minimum(t, last), 0)),
            pl.BlockSpec((1, tk, ATT_WIDTH), lambda b, t: (b, jnp.minimum(t, last), 0)),
            pl.BlockSpec((1, tk, LANES), lambda b, t: (b, jnp.minimum(t, last), 0)),
            pl.BlockSpec((1, SUBLANES, LANES), lambda b, t: (b, 0, 0)),
            const((ATT_WIDTH, LANES)), const((LANES, LANES)), const((tk, tk)),
        ],
        out_specs=[
            pl.BlockSpec((1, h, tk, LANES), lambda b, t: (b, 0, t, 0)),
            pl.BlockSpec((1, h, VT_ROWS, tk), lambda b, t: (b, 0, 0, t)),
            pl.BlockSpec((1, SUBLANES, LANES), lambda b, t: (b, 0, 0)),
            pl.BlockSpec((1, 1, LANES), lambda b, t: (b, 0, 0)),
        ],
        out_shape=[
            jax.ShapeDtypeStruct((batch, h, kv_len, LANES), BF16),
            jax.ShapeDtypeStruct((batch, h, VT_ROWS, kv_len), BF16),
            jax.ShapeDtypeStruct((batch, SUBLANES, LANES), F32),
            jax.ShapeDtypeStruct((batch, 1, LANES), F32),
        ],
        scratch_shapes=[pltpu.VMEM((SUBLANES, LANES), F32)],
        compiler_params=pltpu.CompilerParams(
            dimension_semantics=("arbitrary", "arbitrary"), vmem_limit_bytes=VMEM_LIMIT),
        name="prep_cache",
    )(cache_k, cache_v, cache_logf128, c0, g, selk, ltri)


def _flash_kernel(q_ref, k_ref, vt_ref, o_ref, s0_ref, s1_ref, cm_ref, m_ref, acc_ref,
                  *, tq, tkf, q_off):
    qi = pl.program_id(2)
    q_start = q_off + qi * tq
    n_full = q_start // tkf
    acc_ref[...] = jnp.zeros(acc_ref.shape, F32)
    m_ref[...] = jnp.full(m_ref.shape, NEG_BIG, F32)
    bufs = (s0_ref, s1_ref)

    def scores(k_start, rows, masked, slot):
        for o in range(2):
            kc = k_ref[0, o, pl.ds(k_start, rows), :]
            st = lax.dot_general(kc, q_ref[0, o], (((1,), (1,)), ((), ())),
                                 preferred_element_type=F32)
            if masked:
                st = jnp.where(_row_iota(st.shape) <= _lane_iota(st.shape), st, NEG_BIG)
            bufs[slot][o, 0:rows, :] = st
            cm_ref[slot, o] = jnp.max(st, axis=0, keepdims=True)

    def consume(k_start, rows, slot):
        for o in range(2):
            m_prev = m_ref[o]
            m_new = jnp.maximum(m_prev, cm_ref[slot, o])
            alpha = jnp.exp2(m_prev - m_new)
            p = jnp.exp2(bufs[slot][o, 0:rows, :] - m_new)
            pv = _dot(vt_ref[0, o, :, pl.ds(k_start, rows)], p.astype(BF16))
            acc_ref[o] = alpha * acc_ref[o] + pv
            m_ref[o] = m_new

    full_start = lambda f: pl.multiple_of(f * tkf, tkf)
    diag_start = pl.multiple_of(q_start, tq)
    scores(diag_start, tq, True, 0)

    @pl.when(n_full > 0)
    def _():
        scores(full_start(0), tkf, False, 1)

    consume(diag_start, tq, 0)

    pairs = jnp.maximum(n_full - 1, 0) // 2

    def body(t, carry):
        f = 2 * t
        scores(full_start(f + 1), tkf, False, 0)
        consume(full_start(f), tkf, 1)
        scores(full_start(f + 2), tkf, False, 1)
        consume(full_start(f + 1), tkf, 0)
        return carry

    lax.fori_loop(0, pairs, body, 0)
    f_rem = 2 * pairs
    left = n_full - f_rem

    @pl.when(left == 2)
    def _():
        scores(full_start(f_rem + 1), tkf, False, 0)
        consume(full_start(f_rem), tkf, 1)
        consume(full_start(f_rem + 1), tkf, 0)

    @pl.when(left == 1)
    def _():
        consume(full_start(f_rem), tkf, 1)

    outs = [acc_ref[o, 0:HEAD_DIM, :] / acc_ref[o, HEAD_DIM:HEAD_DIM + 1, :] for o in range(2)]
    o_ref[0] = jnp.concatenate(outs, axis=0).T.astype(BF16)


FLASH_TRIP_KEYS = 4096


def _flash_fixed_kernel(q_ref, k_ref, vt_ref, o_ref, pa_ref, pb_ref, acc_ref, *, tq, tkf, q_off):
    qi = pl.program_id(2)
    q_start = q_off + qi * tq
    n_full = q_start // tkf
    slot_of = lambda f_parity: pa_ref if f_parity else pb_ref

    def probs(k_start, rows, masked, p_ref):
        for o in range(2):
            kc = k_ref[0, o, pl.ds(k_start, rows), :]
            st = lax.dot_general(kc, q_ref[0, o], (((1,), (1,)), ((), ())),
                                 preferred_element_type=F32)
            if masked:
                st = jnp.where(_row_iota(st.shape) <= _lane_iota(st.shape), st, NEG_BIG)
            p_ref[o, 0:rows, :] = jnp.exp2(st).astype(BF16)

    def accumulate(k_start, rows, p_ref, first=False):
        for o in range(2):
            pv = _dot(vt_ref[0, o, :, pl.ds(k_start, rows)], p_ref[o, 0:rows, :])
            acc_ref[o] = pv if first else acc_ref[o] + pv

    full_start = lambda f: pl.multiple_of(f * tkf, tkf)
    diag_start = pl.multiple_of(q_start, tq)
    probs(diag_start, tq, True, pa_ref)

    if tq == tkf:
        start_of = lambda g: pl.multiple_of(jnp.where(g == 0, q_start, (g - 1) * tkf), tkf)
        acc_ref[...] = jnp.zeros(acc_ref.shape, F32)
        n_steps = n_full
        enabled = True

        def step(g, parity):
            probs(full_start(g), tkf, False, slot_of(parity))
            accumulate(start_of(g), tkf, slot_of(1 - parity))

        last = lambda g, parity: accumulate(start_of(g), tkf, slot_of(1 - parity))
    else:
        @pl.when(n_full > 0)
        def _():
            probs(full_start(0), tkf, False, slot_of(0))

        accumulate(diag_start, tq, pa_ref, first=True)
        n_steps = jnp.maximum(n_full - 1, 0)
        enabled = n_full > 0

        def step(f, parity):
            probs(full_start(f + 1), tkf, False, slot_of(1 - parity))
            accumulate(full_start(f), tkf, slot_of(parity))

        last = lambda f, parity: accumulate(full_start(f), tkf, slot_of(parity))

    unroll = max(2, FLASH_TRIP_KEYS // tkf)

    def body(t, carry):
        for i in range(unroll):
            step(unroll * t + i, i % 2)
        return carry

    lax.fori_loop(0, n_steps // unroll, body, 0)
    f0 = (n_steps // unroll) * unroll
    left = n_steps - f0
    for r in range(unroll):
        @pl.when(enabled & (left == r))
        def _(r=r):
            for i in range(r):
                step(f0 + i, i % 2)
            last(f0 + r, r % 2)

    outs = [acc_ref[o, 0:HEAD_DIM, :] / acc_ref[o, HEAD_DIM:HEAD_DIM + 1, :] for o in range(2)]
    o_ref[0] = jnp.concatenate(outs, axis=0).T.astype(BF16)


def _flash(q_aug, k_aug, vt, tq, tkf, q_off, fixed_reference):
    batch, h, tq_total, _ = q_aug.shape
    kv_len = k_aug.shape[2]
    nq = tq_total // tq
    if fixed_reference:
        body = _flash_fixed_kernel
        scratch = [
            pltpu.VMEM((2, max(tq, tkf), tq), BF16),
            pltpu.VMEM((2, max(tq, tkf), tq), BF16),
            pltpu.VMEM((2, VT_ROWS, tq), F32),
        ]
    else:
        body = _flash_kernel
        scratch = [
            pltpu.VMEM((2, max(tq, tkf), tq), F32),
            pltpu.VMEM((2, max(tq, tkf), tq), F32),
            pltpu.VMEM((2, 2, 1, tq), F32),
            pltpu.VMEM((2, 1, tq), F32),
            pltpu.VMEM((2, VT_ROWS, tq), F32),
        ]
    return pl.pallas_call(
        functools.partial(body, tq=tq, tkf=tkf, q_off=q_off),
        grid=(batch, h // 2, nq),
        in_specs=[
            pl.BlockSpec((1, 2, tq, LANES), lambda b, hp, qi: (b, hp, qi, 0)),
            pl.BlockSpec((1, 2, kv_len, LANES), lambda b, hp, qi: (b, hp, 0, 0)),
            pl.BlockSpec((1, 2, VT_ROWS, kv_len), lambda b, hp, qi: (b, hp, 0, 0)),
        ],
        out_specs=pl.BlockSpec((1, tq, LANES), lambda b, hp, qi: (b, qi, hp)),
        out_shape=jax.ShapeDtypeStruct((batch, tq_total, ATT_WIDTH), BF16),
        scratch_shapes=scratch,
        compiler_params=pltpu.CompilerParams(
            dimension_semantics=("arbitrary", "arbitrary", "arbitrary"),
            vmem_limit_bytes=VMEM_LIMIT),
        name="flash_fixed" if fixed_reference else "flash",
    )(q_aug, k_aug, vt)


B_LANE0 = D_INNER
C_LANE0 = D_INNER + N_GROUPS * D_STATE
PK_HI0 = SMALL_DT0
PK_MID0 = PK_HI0 + N_HEADS_SSM
PK_LO0 = PK_MID0 + N_HEADS_SSM


def _pack_head_parts(x):
    hi, mid, lo = _split3(x)
    lane = _lane_iota(x.shape)
    packed = jnp.where(
        lane < PK_MID0, hi.astype(F32),
        jnp.where(lane < PK_LO0, pltpu.roll(mid.astype(F32), N_HEADS_SSM, 1),
                  pltpu.roll(lo.astype(F32), 2 * N_HEADS_SSM, 1)))
    return packed.astype(BF16)


def _head_expand_matrix():
    rows = jnp.arange(LANES)[:, None]
    head = jnp.arange(D_INNER)[None, :] // SSM_HEAD_DIM
    hit = (rows == PK_HI0 + head) | (rows == PK_MID0 + head) | (rows == PK_LO0 + head)
    return jnp.where(hit, 1.0, 0.0).astype(BF16)


def _ssd_kernel(z_ref, xs_ref, bc_ref, small_ref, cpast_ref, h0_ref, cw_ref, cb_ref,
                dtb_ref, alog_ref, dskip_ref, nw_ref, lblk_ref, sel_ref,
                y_ref, cnew_ref, hlast_ref,
                xbuf_ref, ht_ref, ae_ref, dte_ref, ybuf_ref, *, tb, n_valid, tail_row):
    t = pl.program_id(1)
    nt = pl.num_programs(1)
    n_state_blocks = D_INNER // LANES

    n_conv_slabs = CONV_DIM // LANES
    n_inner_slabs = D_INNER // LANES

    @pl.when(t == 0)
    def _():
        for j in range(n_conv_slabs):
            xbuf_ref[j, 0:SUBLANES, :] = cpast_ref[0, :, LANES * j:LANES * (j + 1)]
        for i in range(n_state_blocks):
            ht_ref[:, LANES * i:LANES * (i + 1)] = h0_ref[0, LANES * i:LANES * (i + 1), :].T

    for j in range(n_inner_slabs):
        xbuf_ref[j, SUBLANES:SUBLANES + tb, :] = xs_ref[:, LANES * j:LANES * (j + 1)]
        xbuf_ref[n_inner_slabs + j, SUBLANES:SUBLANES + tb, :] = bc_ref[:, LANES * j:LANES * (j + 1)]

    lane = _lane_iota((tb, LANES))
    row = _row_iota((tb, LANES)) + t * tb
    live = (lane >= SMALL_DT0) & (lane < SMALL_DT0 + N_HEADS_SSM) & (row < n_valid)
    dt = jnp.where(live, _softplus(small_ref[...] + dtb_ref[...]), 0.0)
    a = dt * (-jnp.exp(alog_ref[...]))
    acum = _dot3(lblk_ref[...], a)
    ae_all = _dot(_pack_head_parts(acum), sel_ref[...])
    dte_all = _dot(_pack_head_parts(dt), sel_ref[...])
    for j in range(n_inner_slabs):
        ae_ref[j] = ae_all[:, LANES * j:LANES * (j + 1)]
        dte_ref[j] = dte_all[:, LANES * j:LANES * (j + 1)]

    gw = GROUP_WIDTH
    token_of = lambda r: 32 * (r // 32) + (r // SUBLANES) % 4 + 4 * (r % SUBLANES)
    sub = _row_iota((SSD_CHUNK, gw))
    pos = _lane_iota((SSD_CHUNK, gw)) % SSM_HEAD_DIM
    diag = pos == sub
    causal = token_of(pos) <= token_of(sub)
    blk_row = _row_iota((gw, gw)) // SSM_HEAD_DIM
    blk_col = _lane_iota((gw, gw)) // SSM_HEAD_DIM
    same_head = blk_row == blk_col
    windows = [(32 * half + a) for half in range(2) for a in range(4)]

    def load_perm(ref, j, row0):
        return jnp.concatenate(
            [ref[j, pl.ds(row0 + w, SUBLANES, stride=4), :] for w in windows], axis=0)

    def conv_slab(c0, j):
        cols = slice(LANES * j, LANES * (j + 1))
        out = cb_ref[:, cols]
        for i in range(CONV_W):
            out = out + load_perm(xbuf_ref, j, c0 + SUBLANES - (CONV_W - 1) + i) * cw_ref[i:i + 1, cols]
        return _silu(out)

    def chunk(c, carry):
        c0 = pl.multiple_of(c * SSD_CHUNK, SSD_CHUNK)
        rows = pl.ds(c0, SSD_CHUNK)
        for g in range(N_GROUPS):
            lanes = slice(gw * g, gw * (g + 1))
            j0 = 2 * g
            xs = jnp.concatenate([conv_slab(c0, j0), conv_slab(c0, j0 + 1)], axis=1)
            bm = conv_slab(c0, n_inner_slabs + g)
            cm = conv_slab(c0, n_inner_slabs + N_GROUPS + g)
            ae = jnp.concatenate([load_perm(ae_ref, j0, c0), load_perm(ae_ref, j0 + 1, c0)], axis=1)
            dte = jnp.concatenate([load_perm(dte_ref, j0, c0), load_perm(dte_ref, j0 + 1, c0)], axis=1)
            xdt = xs * dte
            bm16 = bm.astype(BF16)
            cm16 = cm.astype(BF16)

            b4 = jnp.concatenate([bm16] * HEADS_PER_GROUP, axis=0)
            cb4 = lax.dot_general(cm16, b4, (((1,), (1,)), ((), ())), preferred_element_type=F32)
            a_row = jnp.sum(jnp.where(diag, ae, 0.0), axis=0, keepdims=True)
            decay = jnp.exp(jnp.where(causal, ae - a_row, NEG_BIG))
            xdt16 = xdt.astype(BF16)
            x4 = jnp.where(same_head, jnp.concatenate([xdt16] * HEADS_PER_GROUP, axis=0),
                           jnp.zeros((), BF16))
            y = _dot((cb4 * decay).astype(BF16), x4)

            hprev = ht_ref[:, lanes]
            y = y + jnp.exp(ae) * _dot(cm16, hprev.astype(BF16))
            a_last = ae[SSD_CHUNK - 1:SSD_CHUNK, :]
            xw = (xdt * jnp.exp(a_last - ae)).astype(BF16)
            upd = lax.dot_general(bm16, xw, (((0,), (0,)), ((), ())), preferred_element_type=F32)
            ht_ref[:, lanes] = hprev * jnp.exp(a_last) + upd

            y = y + xs * dskip_ref[:, lanes]
            for jj in range(2):
                for v, w in enumerate(windows):
                    ybuf_ref[jj, pl.ds(c0 + w, SUBLANES, stride=4), :] = (
                        y[SUBLANES * v:SUBLANES * (v + 1), LANES * jj:LANES * (jj + 1)])
            y = jnp.concatenate([ybuf_ref[0, rows, :], ybuf_ref[1, rows, :]], axis=1)
            y = y * _silu(z_ref[rows, lanes])
            ms = jnp.mean(y * y, axis=-1, keepdims=True)
            y_ref[rows, lanes] = ((y * lax.rsqrt(ms + EPS)) * nw_ref[:, lanes]).astype(BF16)
        return carry

    lax.fori_loop(0, tb // SSD_CHUNK, chunk, 0)

    for j in range(n_conv_slabs):
        cnew_ref[0, :, LANES * j:LANES * (j + 1)] = xbuf_ref[j, tail_row:tail_row + SUBLANES, :]
        xbuf_ref[j, 0:SUBLANES, :] = xbuf_ref[j, tb:tb + SUBLANES, :]

    @pl.when(t == nt - 1)
    def _():
        for i in range(n_state_blocks):
            hlast_ref[0, LANES * i:LANES * (i + 1), :] = ht_ref[:, LANES * i:LANES * (i + 1)].T


def _ssd(zxg, small, conv_past8, h0, cw, cb, dtb128, alog128, dskip_e, nw, batch, seq, tb, n_valid):
    nt = seq // tb
    rows = batch * seq
    tail_row = n_valid - (seq - tb)
    lblk = _ltri(tb, SSD_CHUNK)
    sel = _head_expand_matrix()
    const = lambda shape: pl.BlockSpec(shape, lambda b, t: tuple(0 for _ in shape))
    return pl.pallas_call(
        functools.partial(_ssd_kernel, tb=tb, n_valid=n_valid, tail_row=tail_row),
        grid=(batch, nt),
        in_specs=[
            pl.BlockSpec((tb, D_INNER), lambda b, t: (b * nt + t, 0)),
            pl.BlockSpec((tb, D_INNER), lambda b, t: (b * nt + t, 1)),
            pl.BlockSpec((tb, D_INNER), lambda b, t: (b * nt + t, 2)),
            pl.BlockSpec((tb, LANES), lambda b, t: (b * nt + t, 0)),
            pl.BlockSpec((1, SUBLANES, CONV_DIM), lambda b, t: (b, 0, 0)),
            pl.BlockSpec((1, D_INNER, D_STATE), lambda b, t: (b, 0, 0)),
            const((CONV_W, CONV_DIM)), const((1, CONV_DIM)),
            const((1, LANES)), const((1, LANES)), const((1, D_INNER)), const((1, D_INNER)),
            const((tb, tb)), const((LANES, D_INNER)),
        ],
        out_specs=[
            pl.BlockSpec((tb, D_INNER), lambda b, t: (b * nt + t, 0)),
            pl.BlockSpec((1, SUBLANES, CONV_DIM), lambda b, t: (b, 0, 0)),
            pl.BlockSpec((1, D_INNER, D_STATE), lambda b, t: (b, 0, 0)),
        ],
        out_shape=[
            jax.ShapeDtypeStruct((rows, D_INNER), BF16),
            jax.ShapeDtypeStruct((batch, SUBLANES, CONV_DIM), F32),
            jax.ShapeDtypeStruct((batch, D_INNER, D_STATE), F32),
        ],
        scratch_shapes=[
            pltpu.VMEM((CONV_DIM // LANES, tb + SUBLANES, LANES), F32),
            pltpu.VMEM((D_STATE, D_INNER), F32),
            pltpu.VMEM((D_INNER // LANES, tb, LANES), F32),
            pltpu.VMEM((D_INNER // LANES, tb, LANES), F32),
            pltpu.VMEM((2, tb, LANES), F32),
        ],
        compiler_params=pltpu.CompilerParams(
            dimension_semantics=("arbitrary", "arbitrary"), vmem_limit_bytes=VMEM_LIMIT),
        name="ssd",
    )(zxg, zxg, zxg, small, conv_past8, h0, cw, cb, dtb128, alog128, dskip_e, nw, lblk, sel)


FF_CHUNK = 1024


def _merge_kernel(att_ref, ys_ref, ga_ref, gs_ref, x_ref, wpa_ref, wpb_ref, wout_ref,
                  n2_ref, wup_ref, wdn_ref, o_ref):
    merged = (_sigmoid(ga_ref[...]) * _dot(att_ref[...], wpa_ref[...])
              + _sigmoid(gs_ref[...]) * _dot(ys_ref[...], wpb_ref[...]))
    x1 = x_ref[...] + _dot(merged.astype(BF16), wout_ref[...])
    ms = jnp.mean(x1 * x1, axis=-1, keepdims=True)
    hn = ((x1 * lax.rsqrt(ms + EPS)) * n2_ref[...]).astype(BF16)
    acc = x1
    for c in range(D_FF // FF_CHUNK):
        cols = slice(FF_CHUNK * c, FF_CHUNK * (c + 1))
        up = jnp.maximum(_dot(hn, wup_ref[:, cols]), 0.0)
        acc = acc + _dot((up * up).astype(BF16), wdn_ref[cols, :])
    o_ref[...] = acc


def _merge(att, ys, zxg, x2d, wpa, wpb, wout, n2, wup, wdn, tm):
    rows = x2d.shape[0]
    ga_blk = 6
    gs_blk = 7
    resident = lambda shape: pl.BlockSpec(shape, lambda i: tuple(0 for _ in shape),
                                          pipeline_mode=pl.Buffered(1))
    return pl.pallas_call(
        _merge_kernel,
        grid=(rows // tm,),
        in_specs=[
            pl.BlockSpec((tm, ATT_WIDTH), lambda i: (i, 0)),
            pl.BlockSpec((tm, D_INNER), lambda i: (i, 0)),
            pl.BlockSpec((tm, D_MODEL), lambda i: (i, ga_blk)),
            pl.BlockSpec((tm, D_MODEL), lambda i: (i, gs_blk)),
            pl.BlockSpec((tm, D_MODEL), lambda i: (i, 0)),
            resident((ATT_WIDTH, D_MODEL)), resident((D_INNER, D_MODEL)),
            resident((D_MODEL, D_MODEL)), resident((1, D_MODEL)),
            resident((D_MODEL, D_FF)), resident((D_FF, D_MODEL)),
        ],
        out_specs=pl.BlockSpec((tm, D_MODEL), lambda i: (i, 0)),
        out_shape=jax.ShapeDtypeStruct((rows, D_MODEL), F32),
        compiler_params=pltpu.CompilerParams(
            dimension_semantics=("arbitrary",), vmem_limit_bytes=VMEM_LIMIT),
        name="merge",
    )(att, ys, zxg, zxg, x2d, wpa, wpb, wout, n2, wup, wdn)


def _pick_tile(n, prefs):
    for p in prefs:
        if n % p == 0:
            return p
    raise ValueError(f"no tile in {prefs} divides {n}")


def _pad_lanes(vec, lane0):
    return jnp.zeros((1, LANES), F32).at[0, lane0:lane0 + vec.shape[0]].set(vec.astype(F32))


def _layer(x, past, p, n_valid):
    batch, seq, _ = x.shape
    rows = batch * seq
    x2d = x.reshape(rows, D_MODEL)

    tm = _pick_tile(rows, (2048, 1024, 512, 256, 128))
    qk, v, zxg, small = _inproj(x2d, p["norm1_w"], p["w_main"], p["w_small"], tm)

    tk = _pick_tile(seq, (512, 256, 128)) if past is None else LANES
    zeros_c = jnp.zeros((batch, SUBLANES, LANES), F32)
    k2_new = HEAD_DIM * jnp.max(jnp.square(p["kw"]))
    if past is None:
        k2 = jnp.full((batch,), k2_new, F32)
    else:
        cache_k, cache_v, cache_logf, conv_past, h0 = past
        n_past = cache_k.shape[1]
        kv_len = n_past + seq
        logf128 = jnp.pad(cache_logf.astype(F32), ((0, 0), (0, 0), (0, LANES - N_HEADS_ATT)))
        tkf = _pick_tile(n_past, (1024, 512, 256, 128))
        k_buf, vt_buf, c_past, k2_cache = _prep_cache(cache_k, cache_v, logf128, zeros_c, tkf, kv_len)
        k2 = jnp.maximum(k2_new, jnp.max(k2_cache, axis=(1, 2)))
    q_norm = jnp.sqrt(HEAD_DIM * jnp.max(jnp.square(p["qw"])))
    shift = (SHIFT_MARGIN * ATT_SCALE * LOG2E) * q_norm * jnp.sqrt(k2)
    shift = shift.astype(BF16).astype(F32)
    shift_rows = jnp.broadcast_to(shift[:, None, None], (batch, SUBLANES, LANES))
    if past is None:
        k_out, logf, q_aug, k_aug, vt, _ = _prep_new(
            qk, v, small, zeros_c, shift_rows, p["bf128"], p["qw"], p["kw"], batch, seq, tk)
        tkf, q_off = tk, 0
        conv_past8 = jnp.zeros((batch, SUBLANES, CONV_DIM), F32)
        h0 = jnp.zeros((batch, D_INNER, D_STATE), F32)
    else:
        k_out, logf, q_aug, k_aug, vt, _ = _prep_new(
            qk, v, small, c_past, shift_rows, p["bf128"], p["qw"], p["kw"], batch, seq, tk,
            bufs=(k_buf, vt_buf), t_off_blocks=n_past // tk)
        q_off = n_past
        conv_past8 = jnp.pad(conv_past.astype(F32), ((0, 0), (SUBLANES - (CONV_W - 1), 0), (0, 0)))
        h0 = h0.astype(F32).reshape(batch, D_INNER, D_STATE)

    att = lax.cond(jnp.max(shift) <= SHIFT_MAX,
                   lambda: _flash(q_aug, k_aug, vt, tk, tkf, q_off, True),
                   lambda: _flash(q_aug, k_aug, vt, tk, tkf, q_off, False))

    tb = _pick_tile(seq, (256, 128))
    ys, conv_new8, h_last = _ssd(zxg, small, conv_past8, h0, p["conv_w"], p["conv_b"],
                                 p["dtb128"], p["alog128"], p["dskip_e"], p["ssm_norm_w"],
                                 batch, seq, tb, n_valid)

    tmm = _pick_tile(rows, (512, 256, 128))
    y = _merge(att.reshape(rows, ATT_WIDTH), ys, zxg, x2d, p["w_pa"], p["w_pb"], p["w_out"],
               p["norm2_w"], p["w_up"], p["w_down"], tmm)

    nv = n_valid
    y = y.reshape(batch, seq, D_MODEL)[:, :nv]
    k_new = k_out.reshape(batch, seq, N_HEADS_ATT, HEAD_DIM)[:, :nv]
    v_new = v.reshape(batch, seq, N_HEADS_ATT, HEAD_DIM)[:, :nv]
    logf_new = logf.reshape(batch, seq, N_HEADS_ATT)[:, :nv]
    conv_new = conv_new8[:, SUBLANES - (CONV_W - 1):, :]
    h_new = h_last.reshape(batch, N_HEADS_SSM, SSM_HEAD_DIM, D_STATE)
    return y, (k_new, v_new, logf_new, conv_new, h_new)


def _layer_params(l, norm1_w, w_in, b_f, q_norm_w, k_norm_w, conv_w, conv_b, dt_bias, A_log,
                  D_skip, ssm_norm_w, w_pa, w_pb, w_out, norm2_w, w_up, w_down):
    w = w_in[l]
    o = 0
    seg = {}
    for name, size in (("q", ATT_WIDTH), ("k", ATT_WIDTH), ("v", ATT_WIDTH), ("f", N_HEADS_ATT),
                       ("z", D_INNER), ("xbc", CONV_DIM), ("dt", N_HEADS_SSM),
                       ("ga", D_MODEL), ("gs", D_MODEL)):
        seg[name] = w[:, o:o + size]
        o += size
    w_main = jnp.concatenate([seg[n] for n in ("q", "k", "v", "z", "xbc", "ga", "gs")], axis=1)
    w_small = jnp.zeros((D_MODEL, LANES), w.dtype)
    w_small = w_small.at[:, SMALL_F0:SMALL_F0 + N_HEADS_ATT].set(seg["f"])
    w_small = w_small.at[:, SMALL_DT0:SMALL_DT0 + N_HEADS_SSM].set(seg["dt"])
    row = lambda a: a.astype(F32).reshape(1, -1)
    return {
        "norm1_w": row(norm1_w[l]),
        "w_main": w_main.astype(BF16),
        "w_small": w_small.astype(BF16),
        "bf128": _pad_lanes(b_f[l], SMALL_F0),
        "qw": row(jnp.tile(q_norm_w[l], N_HEADS_ATT)),
        "kw": row(jnp.tile(k_norm_w[l], N_HEADS_ATT)),
        "conv_w": conv_w[l].astype(F32),
        "conv_b": row(conv_b[l]),
        "dtb128": _pad_lanes(dt_bias[l], SMALL_DT0),
        "alog128": _pad_lanes(A_log[l], SMALL_DT0),
        "dskip_e": row(jnp.repeat(D_skip[l], SSM_HEAD_DIM)),
        "ssm_norm_w": row(ssm_norm_w[l]),
        "w_pa": w_pa[l].astype(BF16),
        "w_pb": w_pb[l].astype(BF16),
        "w_out": w_out[l].astype(BF16),
        "norm2_w": row(norm2_w[l]),
        "w_up": w_up[l].astype(BF16),
        "w_down": w_down[l].astype(BF16),
    }


def kernel(x_prompt, x_sample, cache_k, cache_v, cache_logf, state_conv, state_ssm, norm1_w, w_in,
           b_f, q_norm_w, k_norm_w, conv_w, conv_b, dt_bias, A_log, D_skip, ssm_norm_w, w_pa, w_pb,
           w_out, norm2_w, w_up, w_down):
    depth = w_in.shape[0]
    dec_batch, dec_seq, _ = x_sample.shape
    dec_pad = -(-dec_seq // LANES) * LANES
    yp = x_prompt
    ys = jnp.pad(x_sample, ((0, 0), (0, dec_pad - dec_seq), (0, 0)))
    states_p, states_s = [], []
    for l in range(depth):
        p = _layer_params(l, norm1_w, w_in, b_f, q_norm_w, k_norm_w, conv_w, conv_b, dt_bias,
                          A_log, D_skip, ssm_norm_w, w_pa, w_pb, w_out, norm2_w, w_up, w_down)
        yp, st_p = _layer(yp, None, p, yp.shape[1])
        n_past = cache_k.shape[2]
        past = (cache_k[l].reshape(dec_batch, n_past, ATT_WIDTH),
                cache_v[l].reshape(dec_batch, n_past, ATT_WIDTH),
                cache_logf[l], state_conv[l], state_ssm[l])
        ys_l, st_s = _layer(ys, past, p, dec_seq)
        states_p.append(st_p)
        states_s.append(st_s)
        if l + 1 < depth:
            ys = jnp.pad(ys_l, ((0, 0), (0, dec_pad - dec_seq), (0, 0)))
        else:
            ys = ys_l
    stack = lambda states, i: jnp.stack([s[i] for s in states], axis=0)
    return (yp, ys,
            stack(states_p, 0), stack(states_p, 1), stack(states_p, 2), stack(states_p, 3),
            stack(states_p, 4),
            stack(states_s, 0), stack(states_s, 1), stack(states_s, 2), stack(states_s, 3),
            stack(states_s, 4))
```
